```python
import jax, jax.numpy as jnp
from jax import lax
import numpy as np

D_MODEL = 2048
BATCH = 8
SEQ = 4096
DEPTH = 4

GRID_W = 64
CTX_LEN = 256
N_MIXERS = 3
N_FOURIER = len(range(0, DEPTH, N_MIXERS))
N_MLA = len(range(1, DEPTH, N_MIXERS))
N_RET = len(range(2, DEPTH, N_MIXERS))
N_DENSE = (DEPTH + 1) // 2
N_MOE = DEPTH // 2
EPS = 1e-6
ROPE_BASE = 10000.0

FNET_GROUPS = 8
FNET_GROUP_DIM = D_MODEL // FNET_GROUPS

MLA_HEADS = D_MODEL // 128
MLA_Q_RANK = D_MODEL // 4
MLA_KV_RANK = D_MODEL // 4
MLA_NOPE = 128
MLA_ROPE = 64
MLA_V = 128
MLA_IN = MLA_Q_RANK + MLA_KV_RANK + MLA_ROPE
MLA_SCALE = (MLA_NOPE + MLA_ROPE) ** -0.5
Q_BLOCK = 128

RET_HEADS = 8
RET_DK = D_MODEL // RET_HEADS
RET_DV = 2 * RET_DK
RET_QK = RET_HEADS * RET_DK
RET_VT = RET_HEADS * RET_DV
RET_IN = 2 * RET_QK + 3 * RET_VT
RET_CHUNK = 128

FFN_DIM = 256 * ((8 * D_MODEL // 3 + 255) // 256)
N_EXPERTS = 8
TOP_K = 2
EXPERT_DIM = FFN_DIM

kernel_name = "hybrid_fnet_mla_retention_moe_dit"


def rmsnorm(x, g):
    xf = x.astype(jnp.float32)
    y = xf * lax.rsqrt(jnp.mean(xf * xf, axis=-1, keepdims=True) + EPS)
    return (y * g.astype(jnp.float32)).astype(x.dtype)


def adaln(cond, w, b):
    m = jax.nn.silu(cond) @ w + b
    return [t[:, None, :] for t in jnp.split(m, 6, axis=-1)]


def apply_rope(x, cos, sin):
    xf = x.astype(jnp.float32)
    half = x.shape[-1] // 2
    x1, x2 = xf[..., :half], xf[..., half:]
    return jnp.concatenate([x1 * cos - x2 * sin, x1 * sin + x2 * cos], axis=-1).astype(x.dtype)


def axial_rope_tables(T, dim):
    rows = T // GRID_W
    row = jnp.repeat(jnp.arange(rows, dtype=jnp.float32), GRID_W)
    col = jnp.tile(jnp.arange(GRID_W, dtype=jnp.float32), rows)
    n_freq = dim // 4
    inv = ROPE_BASE ** (-jnp.arange(n_freq, dtype=jnp.float32) / n_freq)
    ang = jnp.concatenate([row[:, None] * inv, col[:, None] * inv], axis=-1)
    return jnp.cos(ang), jnp.sin(ang)


def retention_rope_tables(T, dim):
    inv = ROPE_BASE ** (-jnp.linspace(0.0, 1.0, dim // 2, dtype=jnp.float32))
    ang = jnp.arange(T, dtype=jnp.float32)[:, None] * inv
    return jnp.cos(ang), jnp.sin(ang)


def fourier_mixer(n, w_in, w_out):
    B_, T, _ = n.shape
    u = (n @ w_in).reshape(B_, T, FNET_GROUPS, FNET_GROUP_DIM).astype(jnp.float32)
    f = jnp.fft.fft2(u, axes=(1, 3), norm="ortho").real
    return f.reshape(B_, T, D_MODEL).astype(n.dtype) @ w_out


def mla_project(n, w_in, q_norm, w_uq, kv_norm, w_ukv, cos, sin):
    B_, T, _ = n.shape
    z = n @ w_in
    q_c = z[..., :MLA_Q_RANK]
    kv_c = z[..., MLA_Q_RANK:MLA_Q_RANK + MLA_KV_RANK]
    k_r = z[..., MLA_Q_RANK + MLA_KV_RANK:]
    q = (rmsnorm(q_c, q_norm) @ w_uq).reshape(B_, T, MLA_HEADS, MLA_NOPE + MLA_ROPE)
    kv = (rmsnorm(kv_c, kv_norm) @ w_ukv).reshape(B_, T, MLA_HEADS, MLA_NOPE + MLA_V)
    q_n, q_r = q[..., :MLA_NOPE], q[..., MLA_NOPE:]
    k_n, v = kv[..., :MLA_NOPE], kv[..., MLA_NOPE:]
    if cos is not None:
        q_r = apply_rope(q_r, cos[:, None, :], sin[:, None, :])
        k_r = apply_rope(k_r, cos, sin)
    return q_n, q_r, k_n, k_r, v


def mla_attend(q_n, q_r, k_n, k_r, v):
    s = (jnp.einsum('bqhd,bkhd->bhqk', q_n, k_n)
         + jnp.einsum('bqhr,bkr->bhqk', q_r, k_r)).astype(jnp.float32) * MLA_SCALE
    p = jax.nn.softmax(s, axis=-1).astype(v.dtype)
    return jnp.einsum('bhqk,bkhd->bqhd', p, v)


def mla_mixer(nx, nc, w_in, q_norm, w_uq, kv_norm, w_ukv, w_o, need_ctx_out):
    B_, S, _ = nx.shape
    cos, sin = axial_rope_tables(S, MLA_ROPE)
    cq_n, cq_r, ck_n, ck_r, cv = mla_project(nc, w_in, q_norm, w_uq, kv_norm, w_ukv, None, None)
    xq_n, xq_r, xk_n, xk_r, xv = mla_project(nx, w_in, q_norm, w_uq, kv_norm, w_ukv, cos, sin)
    k_n = jnp.concatenate([xk_n, ck_n], axis=1)
    k_r = jnp.concatenate([xk_r, ck_r], axis=1)
    v = jnp.concatenate([xv, cv], axis=1)
    nb = S // Q_BLOCK

    def blocks(t):
        return jnp.moveaxis(t.reshape(B_, nb, Q_BLOCK, *t.shape[2:]), 1, 0)

    o = lax.map(lambda qs: mla_attend(qs[0], qs[1], k_n, k_r, v), (blocks(xq_n), blocks(xq_r)))
    o = jnp.moveaxis(o, 0, 1).reshape(B_, S, MLA_HEADS * MLA_V)
    yx = o @ w_o
    yc = None
    if need_ctx_out:
        oc = mla_attend(cq_n, cq_r, ck_n, ck_r, cv)
        yc = oc.reshape(B_, nc.shape[1], MLA_HEADS * MLA_V) @ w_o
    return yx, yc


def retention_chunks(q, k, v, log_gamma, state0):
    B_, T, H, _ = q.shape
    dv = v.shape[-1]
    C = RET_CHUNK
    n = T // C

    def chunks(t):
        return t.reshape(B_, n, C, H, t.shape[-1]).transpose(1, 0, 3, 2, 4)

    idx = jnp.arange(C, dtype=jnp.float32)
    diff = idx[:, None] - idx[None, :]
    inner_dec = jnp.where(diff[None] >= 0,
                          jnp.exp(jnp.maximum(diff, 0.0)[None] * log_gamma[:, None, None]), 0.0)
    q_dec = jnp.exp((idx + 1.0)[None] * log_gamma[:, None])[..., None]
    k_dec = jnp.exp((C - 1.0 - idx)[None] * log_gamma[:, None])[..., None]
    c_dec = jnp.exp(C * log_gamma)[:, None, None]

    def step(R, qkv):
        qc, kc, vc = qkv
        a = jnp.einsum('bhid,bhjd->bhij', qc, kc) * inner_dec
        o = jnp.einsum('bhij,bhjv->bhiv', a, vc) + jnp.einsum('bhid,bhdv->bhiv', qc, R) * q_dec
        R = R * c_dec + jnp.einsum('bhjd,bhjv->bhdv', kc * k_dec, vc)
        return R, o

    R, o = lax.scan(step, state0, (chunks(q), chunks(k), chunks(v)))
    o = o.transpose(1, 0, 3, 2, 4).reshape(B_, T, H, dv)
    return o, R


def ret_project(n, w_in, cos, sin):
    B_, T, _ = n.shape
    z = n @ w_in
    q = z[..., :RET_QK].reshape(B_, T, RET_HEADS, RET_DK).astype(jnp.float32)
    k = z[..., RET_QK:2 * RET_QK].reshape(B_, T, RET_HEADS, RET_DK).astype(jnp.float32) * (RET_DK ** -0.5)
    o0 = 2 * RET_QK
    v = z[..., o0:o0 + RET_VT].reshape(B_, T, RET_HEADS, RET_DV).astype(jnp.float32)
    g_f = z[..., o0 + RET_VT:o0 + 2 * RET_VT]
    g_b = z[..., o0 + 2 * RET_VT:]
    if cos is not None:
        q = apply_rope(q, cos[:, None, :], sin[:, None, :])
        k = apply_rope(k, cos[:, None, :], sin[:, None, :])
    return q, k, v, g_f, g_b


def head_groupnorm(o):
    mu = jnp.mean(o, axis=-1, keepdims=True)
    var = jnp.mean(jnp.square(o - mu), axis=-1, keepdims=True)
    return (o - mu) * lax.rsqrt(var + 1e-5)


def retention_mixer(nx, nc, w_in, decay_rate, w_o, need_ctx_out):
    B_, S, _ = nx.shape
    log_gamma = jnp.log1p(-jnp.exp(decay_rate.astype(jnp.float32)))
    cos, sin = retention_rope_tables(S, RET_DK)
    cq, ck, cv, cg_f, cg_b = ret_project(nc, w_in, None, None)
    xq, xk, xv, xg_f, xg_b = ret_project(nx, w_in, cos, sin)
    zero = jnp.zeros((B_, RET_HEADS, RET_DK, RET_DV), jnp.float32)

    def flip(t):
        return jnp.flip(t, axis=1)

    co_f, cs_f = retention_chunks(cq, ck, cv, log_gamma[0], zero)
    co_b, cs_b = retention_chunks(flip(cq), flip(ck), flip(cv), log_gamma[1], zero)
    xo_f, _ = retention_chunks(xq, xk, xv, log_gamma[0], cs_f)
    xo_b, _ = retention_chunks(flip(xq), flip(xk), flip(xv), log_gamma[1], cs_b)

    def combine(o_f, o_b_rev, g_f, g_b):
        B2, T, _, _ = o_f.shape
        o_b = flip(o_b_rev)
        y_f = head_groupnorm(o_f).reshape(B2, T, RET_VT).astype(g_f.dtype)
        y_b = head_groupnorm(o_b).reshape(B2, T, RET_VT).astype(g_b.dtype)
        return (jax.nn.silu(g_f) * y_f + jax.nn.silu(g_b) * y_b) @ w_o

    yx = combine(xo_f, xo_b, xg_f, xg_b)
    yc = combine(co_f, co_b, cg_f, cg_b) if need_ctx_out else None
    return yx, yc


def swiglu(n, w_gu, w_down):
    g, u = jnp.split(n @ w_gu, 2, axis=-1)
    return (jax.nn.silu(g) * u) @ w_down


def moe_swiglu(n, router, w_gu, w_down):
    logits = (n @ router).astype(jnp.float32)
    top_v, top_i = lax.top_k(logits, TOP_K)
    w = jax.nn.softmax(top_v, axis=-1)
    gates = jnp.sum(jax.nn.one_hot(top_i, N_EXPERTS, dtype=jnp.float32) * w[..., None], axis=-2).astype(n.dtype)
    out = jnp.zeros_like(n)
    for e in range(N_EXPERTS):
        out = out + gates[..., e:e + 1] * swiglu(n, w_gu[e], w_down[e])
    return out


def setup_inputs(seed: int = 0) -> dict:
    key = jax.random.key(seed)
    ks = jax.random.split(key, 24)
    f32 = jnp.float32

    def nrm(k, shape, scale):
        return jax.random.normal(k, shape, f32) * scale

    d = D_MODEL
    decay0 = jnp.log(2.0 ** (-5.0 - jnp.arange(RET_HEADS, dtype=f32)))
    return {
        "x": nrm(ks[0], (BATCH, SEQ, d), 1.0),
        "c": nrm(ks[1], (BATCH, d), 1.0),
        "ctx": nrm(ks[2], (BATCH, CTX_LEN, d), 1.0),
        "c_ctx": nrm(ks[3], (d,), 1.0),
        "mod_w": nrm(ks[4], (DEPTH, d, 6 * d), 0.5 * d ** -0.5),
        "mod_b": nrm(ks[5], (DEPTH, 6 * d), 0.02),
        "norm_g": 1.0 + nrm(ks[6], (DEPTH, 2, d), 0.02),
        "final_norm_g": 1.0 + nrm(ks[7], (d,), 0.02),
        "fnet_w_in": nrm(ks[8], (N_FOURIER, d, d), d ** -0.5),
        "fnet_w_out": nrm(ks[9], (N_FOURIER, d, d), d ** -0.5),
        "mla_w_in": nrm(ks[10], (N_MLA, d, MLA_IN), d ** -0.5),
        "mla_q_norm": 1.0 + nrm(ks[11], (N_MLA, MLA_Q_RANK), 0.02),
        "mla_w_uq": nrm(ks[12], (N_MLA, MLA_Q_RANK, MLA_HEADS * (MLA_NOPE + MLA_ROPE)), MLA_Q_RANK ** -0.5),
        "mla_kv_norm": 1.0 + nrm(ks[13], (N_MLA, MLA_KV_RANK), 0.02),
        "mla_w_ukv": nrm(ks[14], (N_MLA, MLA_KV_RANK, MLA_HEADS * (MLA_NOPE + MLA_V)), MLA_KV_RANK ** -0.5),
        "mla_w_o": nrm(ks[15], (N_MLA, MLA_HEADS * MLA_V, d), (MLA_HEADS * MLA_V) ** -0.5),
        "ret_w_in": nrm(ks[16], (N_RET, d, RET_IN), d ** -0.5),
        "ret_decay_rate": decay0 + nrm(ks[17], (N_RET, 2, RET_HEADS), 0.05),
        "ret_w_o": nrm(ks[18], (N_RET, RET_VT, d), RET_VT ** -0.5),
        "ffn_w_gu": nrm(ks[19], (N_DENSE, d, 2 * FFN_DIM), d ** -0.5),
        "ffn_w_down": nrm(ks[20], (N_DENSE, FFN_DIM, d), FFN_DIM ** -0.5),
        "moe_router": nrm(ks[21], (N_MOE, d, N_EXPERTS), d ** -0.5),
        "moe_w_gu": nrm(ks[22], (N_MOE, N_EXPERTS, d, 2 * EXPERT_DIM), d ** -0.5),
        "moe_w_down": nrm(ks[23], (N_MOE, N_EXPERTS, EXPERT_DIM, d), EXPERT_DIM ** -0.5),
    }


def reference(x, c, ctx, c_ctx, mod_w, mod_b, norm_g, final_norm_g, fnet_w_in, fnet_w_out,
              mla_w_in, mla_q_norm, mla_w_uq, mla_kv_norm, mla_w_ukv, mla_w_o,
              ret_w_in, ret_decay_rate, ret_w_o, ffn_w_gu, ffn_w_down,
              moe_router, moe_w_gu, moe_w_down):
    hx, hc = x, ctx
    for i in range(DEPTH):
        need_ctx_out = i < DEPTH - 1
        kind = i % N_MIXERS
        j = i // N_MIXERS
        ctx_used = need_ctx_out or kind != 0
        s_a, sc_a, g_a, s_f, sc_f, g_f = adaln(c, mod_w[i], mod_b[i])
        nx = rmsnorm(hx, norm_g[i, 0]) * (1.0 + sc_a) + s_a
        nc = None
        if ctx_used:
            cs_a, csc_a, cg_a, cs_f, csc_f, cg_f = adaln(c_ctx[None, :], mod_w[i], mod_b[i])
            nc = rmsnorm(hc, norm_g[i, 0]) * (1.0 + csc_a) + cs_a

        if kind == 0:
            yx = fourier_mixer(nx, fnet_w_in[j], fnet_w_out[j])
            yc = fourier_mixer(nc, fnet_w_in[j], fnet_w_out[j]) if need_ctx_out else None
        elif kind == 1:
            yx, yc = mla_mixer(nx, nc, mla_w_in[j], mla_q_norm[j], mla_w_uq[j], mla_kv_norm[j],
                               mla_w_ukv[j], mla_w_o[j], need_ctx_out)
        else:
            yx, yc = retention_mixer(nx, nc, ret_w_in[j], ret_decay_rate[j], ret_w_o[j], need_ctx_out)

        hx = hx + g_a * yx
        if need_ctx_out:
            hc = hc + cg_a * yc

        if i % 2 == 0:
            def ffn(n, m=i // 2):
                return swiglu(n, ffn_w_gu[m], ffn_w_down[m])
        else:
            def ffn(n, m=i // 2):
                return moe_swiglu(n, moe_router[m], moe_w_gu[m], moe_w_down[m])

        nx = rmsnorm(hx, norm_g[i, 1]) * (1.0 + sc_f) + s_f
        hx = hx + g_f * ffn(nx)
        if need_ctx_out:
            nc = rmsnorm(hc, norm_g[i, 1]) * (1.0 + csc_f) + cs_f
            hc = hc + cg_f * ffn(nc)
    return rmsnorm(hx, final_norm_g)
```

```python
import functools
import math

import numpy as np
import jax
import jax.numpy as jnp
from jax import lax
from jax.experimental import pallas as pl
from jax.experimental.pallas import tpu as pltpu

F32 = jnp.float32
BF16 = jnp.bfloat16

VMEM_LIMIT_BYTES = 56 * 1024 * 1024
LANES = 128

EPS = 1e-6
GN_EPS = 1e-5
ROPE_BASE = 10000.0
GRID_W = 64
MOD_ROWS = 16
FNET_GROUP_DIM = 256
MLA_NOPE = 128
MLA_ROPE = 64
MLA_V = 128
MLA_HEAD_PAD = 256
RET_HEADS = 8
CHUNK = 256
TOP_K = 2


def _cparams(*sem):
    return pltpu.CompilerParams(dimension_semantics=sem, vmem_limit_bytes=VMEM_LIMIT_BYTES)


def _pick(dim, pref):
    if dim <= pref:
        return dim
    t = pref
    while t >= LANES:
        if dim % t == 0:
            return t
        t -= LANES
    return dim


def _dot(a, b):
    return jnp.dot(a, b, preferred_element_type=F32)


def _dot_nt(a, b):
    return lax.dot_general(a, b, (((1,), (1,)), ((), ())), preferred_element_type=F32)


def _dot_tn(a, b):
    return lax.dot_general(a, b, (((0,), (0,)), ((), ())), preferred_element_type=F32)


def _silu(x):
    return x / (1.0 + jnp.exp(-x))


def _adaln_kernel(cond_ref, w_ref, b_ref, o_ref):
    a = _silu(cond_ref[...]).astype(BF16)
    o_ref[0] = _dot(a, w_ref[0].astype(BF16)) + b_ref[0]


def _adaln(c, c_ctx, mod_w, mod_b):
    L, D, D6 = mod_w.shape
    B = c.shape[0]
    assert B + 1 <= MOD_ROWS
    cond = jnp.zeros((MOD_ROWS, D), F32).at[:B].set(c).at[B].set(c_ctx)
    bn = _pick(D6, 1024)
    out = pl.pallas_call(
        _adaln_kernel,
        grid=(L, D6 // bn),
        in_specs=[pl.BlockSpec((MOD_ROWS, D), lambda l, j: (0, 0)),
                  pl.BlockSpec((1, D, bn), lambda l, j: (l, 0, j)),
                  pl.BlockSpec((1, 1, bn), lambda l, j: (l, 0, j))],
        out_specs=pl.BlockSpec((1, MOD_ROWS, bn), lambda l, j: (l, 0, j)),
        out_shape=jax.ShapeDtypeStruct((L, MOD_ROWS, D6), F32),
        compiler_params=_cparams("parallel", "parallel"),
        name="adaln",
    )(cond, mod_w, mod_b.reshape(L, 1, D6))
    return out.reshape(L * MOD_ROWS * 6, 1, D)


class _Rows:
    def __init__(self, B, S, Tc):
        self.B, self.S, self.Tc = B, S, Tc
        self.ML, self.MC = B * S, B * Tc
        self.M = self.ML + self.MC

    def tile(self, pref):
        return _pick(math.gcd(self.S, self.MC), pref)

    def mod_index(self, layer, which, row):
        mi = jnp.where(row >= self.ML, self.B, row // self.S)
        return (layer * MOD_ROWS + mi) * 6 + which


def _norm_mod(h_ref, g_ref, sc_ref, sh_ref):
    x = h_ref[...]
    y = x * lax.rsqrt(jnp.mean(x * x, axis=-1, keepdims=True) + EPS)
    return (y * g_ref[...]) * (1.0 + sc_ref[0]) + sh_ref[0]


def _norm_kernel(h_ref, g_ref, sc_ref, sh_ref, o_ref):
    o_ref[...] = _norm_mod(h_ref, g_ref, sc_ref, sh_ref).astype(o_ref.dtype)


def _norm_router_kernel(h_ref, g_ref, sc_ref, sh_ref, rhi_ref, rlo_ref, o_ref, gates_ref, *, n_experts):
    n = _norm_mod(h_ref, g_ref, sc_ref, sh_ref)
    nh = n.astype(BF16)
    o_ref[...] = nh
    nl = (n - nh.astype(F32)).astype(BF16)
    lg = _dot(nh, rhi_ref[...]) + (_dot(nl, rhi_ref[...]) + _dot(nh, rlo_ref[...]))
    lane = lax.broadcasted_iota(jnp.int32, lg.shape, 1).astype(F32)
    neg = jnp.float32(-jnp.inf)
    lg = jnp.where(lane < n_experts, lg, neg)
    m1 = jnp.max(lg, axis=-1, keepdims=True)
    i1 = jnp.min(jnp.where(lg == m1, lane, float(LANES)), axis=-1, keepdims=True)
    sel1 = lane == i1
    lg2 = jnp.where(sel1, neg, lg)
    m2 = jnp.max(lg2, axis=-1, keepdims=True)
    i2 = jnp.min(jnp.where(lg2 == m2, lane, float(LANES)), axis=-1, keepdims=True)
    sel2 = lane == i2
    e2 = jnp.exp(m2 - m1)
    w1 = 1.0 / (1.0 + e2)
    w2 = e2 / (1.0 + e2)
    gates_ref[...] = jnp.where(sel1, w1, 0.0) + jnp.where(sel2, w2, 0.0)


def _norm(h, g, mod, rows, layer, sh_idx, sc_idx, m_act, router=None):
    D = h.shape[1]
    bm = rows.tile(512)
    row_spec = pl.BlockSpec((bm, D), lambda i: (i, 0))
    in_specs = [row_spec,
                pl.BlockSpec((1, D), lambda i: (0, 0)),
                pl.BlockSpec((1, 1, D), lambda i: (rows.mod_index(layer, sc_idx, i * bm), 0, 0)),
                pl.BlockSpec((1, 1, D), lambda i: (rows.mod_index(layer, sh_idx, i * bm), 0, 0))]
    args = [h, g.reshape(1, D), mod, mod]
    if router is None:
        return pl.pallas_call(
            _norm_kernel, grid=(m_act // bm,), in_specs=in_specs, out_specs=row_spec,
            out_shape=jax.ShapeDtypeStruct((m_act, D), BF16),
            compiler_params=_cparams("parallel"), name="norm_mod",
        )(*args)
    n_experts = router.shape[1]
    rpad = jnp.zeros((D, LANES), F32).at[:, :n_experts].set(router)
    rhi = rpad.astype(BF16)
    rlo = (rpad - rhi.astype(F32)).astype(BF16)
    full = pl.BlockSpec((D, LANES), lambda i: (0, 0))
    return pl.pallas_call(
        functools.partial(_norm_router_kernel, n_experts=n_experts),
        grid=(m_act // bm,), in_specs=in_specs + [full, full],
        out_specs=[row_spec, pl.BlockSpec((bm, LANES), lambda i: (i, 0))],
        out_shape=[jax.ShapeDtypeStruct((m_act, D), BF16), jax.ShapeDtypeStruct((m_act, LANES), F32)],
        compiler_params=_cparams("parallel"), name="norm_mod_router",
    )(*args, rhi, rlo)


def _final_norm_kernel(h_ref, g_ref, o_ref):
    x = h_ref[...]
    y = x * lax.rsqrt(jnp.mean(x * x, axis=-1, keepdims=True) + EPS)
    o_ref[...] = y * g_ref[...]


def _final_norm(h, g, m_act):
    D = h.shape[1]
    bm = _pick(m_act, 512)
    return pl.pallas_call(
        _final_norm_kernel, grid=(m_act // bm,),
        in_specs=[pl.BlockSpec((bm, D), lambda i: (i, 0)), pl.BlockSpec((1, D), lambda i: (0, 0))],
        out_specs=pl.BlockSpec((bm, D), lambda i: (i, 0)),
        out_shape=jax.ShapeDtypeStruct((m_act, D), F32),
        compiler_params=_cparams("parallel"), name="final_norm",
    )(h, g.reshape(1, D))


def _mm_kernel(a_ref, w_ref, o_ref, *acc, nk):
    if nk == 1:
        o_ref[...] = _dot(a_ref[...], w_ref[...]).astype(o_ref.dtype)
        return
    acc_ref, = acc
    k = pl.program_id(2)

    @pl.when(k == 0)
    def _():
        acc_ref[...] = jnp.zeros_like(acc_ref)

    acc_ref[...] += _dot(a_ref[...], w_ref[...])

    @pl.when(k == nk - 1)
    def _():
        o_ref[...] = acc_ref[...].astype(o_ref.dtype)


def _mm(a, w, out_dtype, m_act, bm=1024, bn=1024, bk=None):
    K, N = w.shape
    bm, bn = _pick(m_act, bm), _pick(N, bn)
    bk = K if bk is None else _pick(K, bk)
    nk = K // bk
    return pl.pallas_call(
        functools.partial(_mm_kernel, nk=nk),
        grid=(N // bn, m_act // bm, nk),
        in_specs=[pl.BlockSpec((bm, bk), lambda j, i, k: (i, k)),
                  pl.BlockSpec((bk, bn), lambda j, i, k: (k, j))],
        out_specs=pl.BlockSpec((bm, bn), lambda j, i, k: (i, j)),
        out_shape=jax.ShapeDtypeStruct((m_act, N), out_dtype),
        scratch_shapes=[pltpu.VMEM((bm, bn), F32)] if nk > 1 else [],
        compiler_params=_cparams("parallel", "parallel", "arbitrary"), name="mm",
    )(a, w)


def _mm_res_kernel(a_ref, w_ref, h_ref, gate_ref, o_ref, *acc, nk):
    if nk == 1:
        o_ref[...] = h_ref[...] + gate_ref[0] * _dot(a_ref[...], w_ref[...])
        return
    acc_ref, = acc
    k = pl.program_id(2)

    @pl.when(k == 0)
    def _():
        acc_ref[...] = jnp.zeros_like(acc_ref)

    acc_ref[...] += _dot(a_ref[...], w_ref[...])

    @pl.when(k == nk - 1)
    def _():
        o_ref[...] = h_ref[...] + gate_ref[0] * acc_ref[...]


def _mm_res(a, w, h, mod, rows, layer, gate_idx, m_rows, h_row_off=0, bm=1024, bn=1024, bk=None):
    K, N = w.shape
    bm, bn = _pick(math.gcd(rows.S, rows.MC, m_rows, h_row_off or m_rows), bm), _pick(N, bn)
    bk = K if bk is None else _pick(K, bk)
    nk = K // bk
    off = h_row_off // bm
    return pl.pallas_call(
        functools.partial(_mm_res_kernel, nk=nk),
        grid=(N // bn, m_rows // bm, nk),
        in_specs=[pl.BlockSpec((bm, bk), lambda j, i, k: (i, k)),
                  pl.BlockSpec((bk, bn), lambda j, i, k: (k, j)),
                  pl.BlockSpec((bm, bn), lambda j, i, k: (i + off, j)),
                  pl.BlockSpec((1, 1, bn),
                               lambda j, i, k: (rows.mod_index(layer, gate_idx, (i + off) * bm), 0, j))],
        out_specs=pl.BlockSpec((bm, bn), lambda j, i, k: (i + off, j)),
        out_shape=jax.ShapeDtypeStruct(h.shape, F32),
        scratch_shapes=[pltpu.VMEM((bm, bn), F32)] if nk > 1 else [],
        input_output_aliases={2: 0},
        compiler_params=_cparams("parallel", "parallel", "arbitrary"), name="mm_residual",
    )(a, w, h, mod)


def _swiglu_kernel(a_ref, wg_ref, wu_ref, *rest, gated):
    a = a_ref[...]
    r = _silu(_dot(a, wg_ref[0])) * _dot(a, wu_ref[0])
    if gated:
        gates_ref, o_ref = rest
        gt = gates_ref[...]
        lane = lax.broadcasted_iota(jnp.int32, gt.shape, 1)
        col = jnp.sum(jnp.where(lane == pl.program_id(0), gt, 0.0), axis=-1, keepdims=True)
        r = r * col
    else:
        o_ref, = rest
    o_ref[...] = r.astype(o_ref.dtype)


def _swiglu(n, w_gu, m_act, gates=None, bm=1024, bn=512):
    E, D, F2 = w_gu.shape
    F = F2 // 2
    bm, bn = _pick(m_act, bm), _pick(F, bn)
    nf = F // bn
    in_specs = [pl.BlockSpec((bm, D), lambda e, j, i: (i, 0)),
                pl.BlockSpec((1, D, bn), lambda e, j, i: (e, 0, j)),
                pl.BlockSpec((1, D, bn), lambda e, j, i: (e, 0, nf + j))]
    args = [n, w_gu, w_gu]
    if gates is not None:
        in_specs.append(pl.BlockSpec((bm, LANES), lambda e, j, i: (i, 0)))
        args.append(gates)
    return pl.pallas_call(
        functools.partial(_swiglu_kernel, gated=gates is not None),
        grid=(E, nf, m_act // bm),
        in_specs=in_specs,
        out_specs=pl.BlockSpec((bm, bn), lambda e, j, i: (i, e * nf + j)),
        out_shape=jax.ShapeDtypeStruct((m_act, E * F), BF16),
        compiler_params=_cparams("parallel", "parallel", "parallel"), name="swiglu",
    )(*args)


def _dft_cos_sin(n):
    k = np.arange(n, dtype=np.int64)
    ang = 2.0 * np.pi * ((k[:, None] * k[None, :]) % n).astype(np.float64) / n
    return np.cos(ang), np.sin(ang)


def _fnet_chan_kernel(u_ref, w_ref, o_ref, *, groups, gd):
    for g in range(groups):
        r = _dot(u_ref[:, g * gd:(g + 1) * gd], w_ref[...])
        o_ref[0, :, g * gd:(g + 1) * gd] = r[:, :gd].astype(o_ref.dtype)
        o_ref[1, :, g * gd:(g + 1) * gd] = r[:, gd:].astype(o_ref.dtype)


def _fnet_chan(u, m_act):
    D = u.shape[1]
    gd = FNET_GROUP_DIM
    c, s = _dft_cos_sin(gd)
    w = jnp.asarray(np.concatenate([c, -s], axis=1) / math.sqrt(gd), BF16)
    bm = _pick(m_act, 512)
    return pl.pallas_call(
        functools.partial(_fnet_chan_kernel, groups=D // gd, gd=gd),
        grid=(m_act // bm,),
        in_specs=[pl.BlockSpec((bm, D), lambda i: (i, 0)), pl.BlockSpec((gd, 2 * gd), lambda i: (0, 0))],
        out_specs=pl.BlockSpec((2, bm, D), lambda i: (0, i, 0)),
        out_shape=jax.ShapeDtypeStruct((2, m_act, D), BF16),
        compiler_params=_cparams("parallel"), name="fnet_chan_dft",
    )(u, w)


def _fft1_kernel(x_ref, w_ref, tw_ref, o_ref, *, n1, nb, D):
    x2 = x_ref[...].reshape(2 * n1, nb * D)
    y = _dot(w_ref[...], x2)
    yr, yi = y[:n1], y[n1:]
    for t in range(nb):
        twr = tw_ref[0, 0, :, t:t + 1]
        twi = tw_ref[1, 0, :, t:t + 1]
        a, b = yr[:, t * D:(t + 1) * D], yi[:, t * D:(t + 1) * D]
        o_ref[0, 0, t] = (a * twr - b * twi).astype(o_ref.dtype)
        o_ref[0, 1, t] = (a * twi + b * twr).astype(o_ref.dtype)


def _fft3_kernel(z_ref, w_ref, o_ref, *, n2, nb, D):
    for t in range(nb):
        z = z_ref[0, :, :, t * D:(t + 1) * D].reshape(2 * n2, D)
        o_ref[0, :, t * D:(t + 1) * D] = _dot(w_ref[...], z).astype(o_ref.dtype)


def _fnet_seq_latent(p, rows):
    D = p.shape[2]
    B, S = rows.B, rows.S
    n1 = 64
    n2 = S // n1
    assert n1 * n2 == S and n2 % 8 == 0 and rows.MC % n2 == 0
    nb = 8
    c1, s1 = _dft_cos_sin(n1)
    w1 = jnp.asarray(np.block([[c1, s1], [-s1, c1]]) / math.sqrt(n1), BF16)
    f = np.arange(n1)[:, None] * np.arange(n2)[None, :]
    ang = 2.0 * np.pi * f / S
    tw = np.stack([np.cos(ang), -np.sin(ang)]).reshape(2, n1, n2 // nb, nb).transpose(0, 2, 1, 3)
    tw = jnp.asarray(tw, F32)
    pv = p.reshape(2, p.shape[1] // n2, n2 * D)
    y = pl.pallas_call(
        functools.partial(_fft1_kernel, n1=n1, nb=nb, D=D),
        grid=(B, n2 // nb),
        in_specs=[pl.BlockSpec((2, n1, nb * D), lambda b, j: (0, b, j)),
                  pl.BlockSpec((2 * n1, 2 * n1), lambda b, j: (0, 0)),
                  pl.BlockSpec((2, 1, n1, nb), lambda b, j: (0, j, 0, 0))],
        out_specs=pl.BlockSpec((1, 2, nb, n1, D), lambda b, j: (b, 0, j, 0, 0)),
        out_shape=jax.ShapeDtypeStruct((B, 2, n2, n1, D), BF16),
        compiler_params=_cparams("parallel", "parallel"), name="fnet_fft_stage1",
    )(pv, w1, tw)
    c2, s2 = _dft_cos_sin(n2)
    w3 = jnp.asarray(np.concatenate([c2, s2], axis=1) / math.sqrt(n2), BF16)
    yv = y.reshape(B, 2, n2, n1 * D)
    out = pl.pallas_call(
        functools.partial(_fft3_kernel, n2=n2, nb=nb, D=D),
        grid=(B, n1 // nb),
        in_specs=[pl.BlockSpec((1, 2, n2, nb * D), lambda b, j: (b, 0, 0, j)),
                  pl.BlockSpec((n2, 2 * n2), lambda b, j: (0, 0))],
        out_specs=pl.BlockSpec((1, n2, nb * D), lambda b, j: (b, 0, j)),
        out_shape=jax.ShapeDtypeStruct((B, n2, n1 * D), BF16),
        compiler_params=_cparams("parallel", "parallel"), name="fnet_fft_stage2",
    )(yv, w3)
    return out.reshape(B * S, D)


def _dft_ctx_kernel(z_ref, w_ref, o_ref, *, T, D):
    z = z_ref[...].reshape(2 * T, D)
    o_ref[...] = _dot(w_ref[...], z).astype(o_ref.dtype)


def _fnet_seq_ctx(p, rows):
    D = p.shape[2]
    T = rows.Tc
    c, s = _dft_cos_sin(T)
    w = jnp.asarray(np.concatenate([c, s], axis=1) / math.sqrt(T), BF16)
    off = rows.ML // T
    return pl.pallas_call(
        functools.partial(_dft_ctx_kernel, T=T, D=D),
        grid=(rows.B,),
        in_specs=[pl.BlockSpec((2, T, D), lambda b: (0, off + b, 0)),
                  pl.BlockSpec((T, 2 * T), lambda b: (0, 0))],
        out_specs=pl.BlockSpec((T, D), lambda b: (b, 0)),
        out_shape=jax.ShapeDtypeStruct((rows.MC, D), BF16),
        compiler_params=_cparams("parallel"), name="fnet_dft_ctx",
    )(p, w)


def _fourier_mixer(n, h, w_in, w_out, mod, rows, layer, need_ctx):
    m_act = rows.M if need_ctx else rows.ML
    u = _mm(n, w_in.astype(BF16), BF16, m_act)
    p = _fnet_chan(u, m_act)
    f_lat = _fnet_seq_latent(p, rows)
    w_out = w_out.astype(BF16)
    h = _mm_res(f_lat, w_out, h, mod, rows, layer, 2, rows.ML)
    if need_ctx:
        f_ctx = _fnet_seq_ctx(p, rows)
        h = _mm_res(f_ctx, w_out, h, mod, rows, layer, 2, rows.MC, h_row_off=rows.ML)
    return h


def _mla_proj_kernel(z_ref, qn_ref, kvn_ref, wq_ref, wqs_ref, wkv_ref, cos_ref, sin_ref,
                     q_ref, k_ref, v_ref, *, H, qr, kvr, scale):
    def rms(x, g):
        return ((x * lax.rsqrt(jnp.mean(x * x, axis=-1, keepdims=True) + EPS)) * g).astype(BF16)

    qn = rms(z_ref[:, :qr], qn_ref[...])
    kvn = rms(z_ref[:, qr:qr + kvr], kvn_ref[...])
    kr = z_ref[:, qr + kvr:qr + kvr + LANES]
    kr_sw = z_ref[:, qr + kvr + LANES:qr + kvr + 2 * LANES]
    cos, sin = cos_ref[...], sin_ref[...]
    k_rope = (kr * cos + kr_sw * sin).astype(BF16)
    q = _dot(qn, wq_ref[...]) * scale
    q_sw = _dot(qn, wqs_ref[...]) * scale
    kv = _dot(kvn, wkv_ref[...])
    P = MLA_HEAD_PAD
    for hh in range(H):
        q_ref[0, hh, :, :LANES] = q[:, hh * P:hh * P + LANES].astype(BF16)
        q_ref[0, hh, :, LANES:] = (q[:, hh * P + LANES:(hh + 1) * P] * cos
                                   + q_sw[:, hh * LANES:(hh + 1) * LANES] * sin).astype(BF16)
        k_ref[0, hh, :, :LANES] = kv[:, hh * LANES:(hh + 1) * LANES].astype(BF16)
        k_ref[0, hh, :, LANES:] = k_rope
        v_ref[0, hh] = kv[:, (H + hh) * LANES:(H + hh + 1) * LANES].astype(BF16)


def _mla_attn_kernel(q_ref, k_ref, v_ref, o_ref, *, S, n_lat_blocks):
    def attend(k, v):
        s = _dot_nt(q_ref[0, 0], k)
        m = jnp.max(s, axis=-1, keepdims=True)
        p = jnp.exp(s - m)
        l = jnp.sum(p, axis=-1, keepdims=True)
        o_ref[...] = (_dot(p.astype(BF16), v) / l).astype(o_ref.dtype)

    i = pl.program_id(2)

    @pl.when(i < n_lat_blocks)
    def _():
        attend(k_ref[0, 0], v_ref[0, 0])

    @pl.when(i >= n_lat_blocks)
    def _():
        attend(k_ref[0, 0, S:, :], v_ref[0, 0, S:, :])


def _axial_rope_table(S, Tc):
    half = MLA_ROPE // 2
    rows_ = S // GRID_W
    row = jnp.repeat(jnp.arange(rows_, dtype=F32), GRID_W)
    col = jnp.tile(jnp.arange(GRID_W, dtype=F32), rows_)
    n_freq = MLA_ROPE // 4
    inv = ROPE_BASE ** (-jnp.arange(n_freq, dtype=F32) / n_freq)
    ang = jnp.concatenate([row[:, None] * inv, col[:, None] * inv], axis=-1)
    cos, sin = jnp.cos(ang), jnp.sin(ang)
    zpad = jnp.zeros((S, LANES - 2 * half), F32)
    cos_l = jnp.concatenate([cos, cos, zpad], axis=-1)
    sin_l = jnp.concatenate([sin, sin, zpad], axis=-1)
    cos_c = jnp.concatenate([jnp.ones((Tc, 2 * half), F32), jnp.zeros((Tc, LANES - 2 * half), F32)], axis=-1)
    return jnp.concatenate([cos_l, cos_c], axis=0), jnp.concatenate([sin_l, jnp.zeros((Tc, LANES), F32)], axis=0)


def _swap_halves_neg(w):
    half = w.shape[-1] // 2
    return jnp.concatenate([-w[..., half:], w[..., :half]], axis=-1)


def _mla_mixer(n, h, w_in, q_norm, w_uq, kv_norm, w_ukv, w_o, mod, rows, layer):
    B, S, Tc = rows.B, rows.S, rows.Tc
    D = n.shape[1]
    qr, kvr = q_norm.shape[0], kv_norm.shape[0]
    H = w_uq.shape[1] // (MLA_NOPE + MLA_ROPE)
    C = CHUNK
    assert Tc == C and S % C == 0 and MLA_NOPE == LANES and MLA_V == LANES
    pad = LANES - MLA_ROPE
    w_kr = w_in[:, qr + kvr:]
    zc = jnp.zeros((D, pad), F32)
    w_in_p = jnp.concatenate([w_in[:, :qr + kvr], w_kr, zc, _swap_halves_neg(w_kr), zc], axis=1).astype(BF16)
    wq = w_uq.reshape(qr, H, MLA_NOPE + MLA_ROPE)
    zq = jnp.zeros((qr, H, pad), F32)
    wq_p = jnp.concatenate([wq, zq], axis=-1).reshape(qr, H * MLA_HEAD_PAD).astype(BF16)
    wq_sw = jnp.concatenate([_swap_halves_neg(wq[..., MLA_NOPE:]), zq], axis=-1).reshape(qr, H * LANES).astype(BF16)
    wkv = w_ukv.reshape(kvr, H, MLA_NOPE + MLA_V)
    wkv_p = jnp.concatenate([wkv[..., :MLA_NOPE].reshape(kvr, H * LANES),
                             wkv[..., MLA_NOPE:].reshape(kvr, H * LANES)], axis=1).astype(BF16)
    cos_t, sin_t = _axial_rope_table(S, Tc)

    z = _mm(n, w_in_p, F32, rows.M)
    nS = S // C
    n_lat_tiles = rows.ML // C
    T = S + Tc

    def bt(i):
        lat = i < n_lat_tiles
        return jnp.where(lat, i // nS, i - n_lat_tiles), jnp.where(lat, i % nS, nS)

    def kv_map(i):
        b, t = bt(i)
        return (b, 0, t, 0)

    ZW = w_in_p.shape[1]
    const = lambda shape: pl.BlockSpec(shape, lambda i: (0,) * len(shape))
    q, k, v = pl.pallas_call(
        functools.partial(_mla_proj_kernel, H=H, qr=qr, kvr=kvr,
                          scale=float((MLA_NOPE + MLA_ROPE) ** -0.5)),
        grid=(rows.M // C,),
        in_specs=[pl.BlockSpec((C, ZW), lambda i: (i, 0)),
                  const((1, qr)), const((1, kvr)),
                  const(wq_p.shape), const(wq_sw.shape), const(wkv_p.shape),
                  pl.BlockSpec((C, LANES), lambda i: (bt(i)[1], 0)),
                  pl.BlockSpec((C, LANES), lambda i: (bt(i)[1], 0))],
        out_specs=[pl.BlockSpec((1, H, C, MLA_HEAD_PAD), kv_map),
                   pl.BlockSpec((1, H, C, MLA_HEAD_PAD), kv_map),
                   pl.BlockSpec((1, H, C, MLA_V), kv_map)],
        out_shape=[jax.ShapeDtypeStruct((B, H, T, MLA_HEAD_PAD), BF16),
                   jax.ShapeDtypeStruct((B, H, T, MLA_HEAD_PAD), BF16),
                   jax.ShapeDtypeStruct((B, H, T, MLA_V), BF16)],
        compiler_params=_cparams("parallel"), name="mla_project",
    )(z, q_norm.reshape(1, qr), kv_norm.reshape(1, kvr), wq_p, wq_sw, wkv_p, cos_t, sin_t)

    bq = C
    o = pl.pallas_call(
        functools.partial(_mla_attn_kernel, S=S, n_lat_blocks=nS),
        grid=(B, H, nS + 1),
        in_specs=[pl.BlockSpec((1, 1, bq, MLA_HEAD_PAD), lambda b, hh, i: (b, hh, i, 0)),
                  pl.BlockSpec((1, 1, T, MLA_HEAD_PAD), lambda b, hh, i: (b, hh, 0, 0)),
                  pl.BlockSpec((1, 1, T, MLA_V), lambda b, hh, i: (b, hh, 0, 0))],
        out_specs=pl.BlockSpec((bq, MLA_V),
                               lambda b, hh, i: (jnp.where(i < nS, b * nS + i, n_lat_tiles + b), hh)),
        out_shape=jax.ShapeDtypeStruct((rows.M, H * MLA_V), BF16),
        compiler_params=_cparams("parallel", "parallel", "arbitrary"), name="mla_attention",
    )(q, k, v)
    return _mm_res(o, w_o.astype(BF16), h, mod, rows, layer, 2, rows.M)


def _ret_kernel(dr_ref, q_ref, k_ref, v_ref, cos_ref, sin_ref, *rest, H, C, dk, dv, backward, combine):
    if combine:
        yf_ref, gf_ref, gb_ref, o_ref, R_ref, D_ref, qd_ref, kd_ref, cd_ref = rest
    else:
        o_ref, R_ref, D_ref, qd_ref, kd_ref, cd_ref = rest
    b, s = pl.program_id(0), pl.program_id(1)
    half = dk // 2

    @pl.when((b == 0) & (s == 0))
    def _():
        ri = lax.broadcasted_iota(jnp.int32, (C, C), 0).astype(F32)
        ci = lax.broadcasted_iota(jnp.int32, (C, C), 1).astype(F32)
        diff = (ci - ri) if backward else (ri - ci)
        r = lax.broadcasted_iota(jnp.int32, (C, LANES), 0).astype(F32)
        qpow = (C - r) if backward else (r + 1.0)
        kpow = r if backward else (C - 1.0 - r)
        for hh in range(H):
            lg = jnp.log1p(-jnp.exp(dr_ref[hh]))
            D_ref[hh] = jnp.where(diff >= 0, jnp.exp(jnp.maximum(diff, 0.0) * lg[0:1, :]), 0.0)
            qd_ref[hh] = jnp.exp(qpow * lg[0:1, :LANES])
            kd_ref[hh] = jnp.exp(kpow * lg[0:1, :LANES])
            cd_ref[hh] = jnp.exp(float(C) * lg[:, :LANES])

    @pl.when(s == 0)
    def _():
        R_ref[...] = jnp.zeros_like(R_ref)

    cos, sin = cos_ref[...], sin_ref[...]

    def rope(x):
        x1, x2 = x[:, :half], x[:, half:]
        return jnp.concatenate([x1 * cos - x2 * sin, x1 * sin + x2 * cos], axis=-1)

    for hh in range(H):
        q = rope(q_ref[:, hh * dk:(hh + 1) * dk].astype(F32))
        k = rope(k_ref[:, hh * dk:(hh + 1) * dk].astype(F32)) * (dk ** -0.5)
        v = v_ref[:, hh * dv:(hh + 1) * dv]
        qd = jnp.concatenate([qd_ref[hh]] * (dk // LANES), axis=-1)
        kd = jnp.concatenate([kd_ref[hh]] * (dk // LANES), axis=-1)
        cd = jnp.concatenate([cd_ref[hh][0:1]] * (dv // LANES), axis=-1)
        a = _dot_nt(q.astype(BF16), k.astype(BF16)) * D_ref[hh]
        R = R_ref[hh]
        o = _dot(a.astype(BF16), v) + _dot((q * qd).astype(BF16), R.astype(BF16))
        R_ref[hh] = R * cd + _dot_tn((k * kd).astype(BF16), v)
        mu = jnp.mean(o, axis=-1, keepdims=True)
        d = o - mu
        y = d * lax.rsqrt(jnp.mean(d * d, axis=-1, keepdims=True) + GN_EPS)
        sl = slice(hh * dv, (hh + 1) * dv)
        if combine:
            y = (_silu(gf_ref[:, sl].astype(F32)) * yf_ref[:, sl].astype(F32)
                 + _silu(gb_ref[:, sl].astype(F32)) * y)
        o_ref[:, sl] = y.astype(o_ref.dtype)


def _retention_rope_table(S, Tc, dim):
    inv = ROPE_BASE ** (-jnp.linspace(0.0, 1.0, dim // 2, dtype=F32))
    ang = jnp.arange(S, dtype=F32)[:, None] * inv
    cos = jnp.concatenate([jnp.cos(ang), jnp.ones((Tc, dim // 2), F32)], axis=0)
    sin = jnp.concatenate([jnp.sin(ang), jnp.zeros((Tc, dim // 2), F32)], axis=0)
    return cos, sin


def _retention_mixer(n, h, w_in, decay_rate, w_o, mod, rows, layer):
    B, S, Tc = rows.B, rows.S, rows.Tc
    D = n.shape[1]
    H = RET_HEADS
    dk = D // H
    dv = 2 * dk
    QK, VT = H * dk, H * dv
    C = CHUNK
    assert Tc == C and S % C == 0 and dk // 2 == LANES and w_in.shape[1] == 2 * QK + 3 * VT and VT == 2 * QK
    z = _mm(n, w_in.astype(BF16), BF16, rows.M)
    cos_t, sin_t = _retention_rope_table(S, Tc, dk)
    nS = S // C
    n_lat_tiles = rows.ML // C
    dr = jnp.broadcast_to(decay_rate.astype(F32)[:, :, None, None], (2, H, 8, C))

    def run(backward, yf):
        def blk(b, s):
            j = (nS - s) if backward else (s - 1)
            return jnp.where(s == 0, n_lat_tiles + b, b * nS + j)

        def tab(b, s):
            j = (nS - s) if backward else (s - 1)
            return (jnp.where(s == 0, nS, j), 0)

        in_specs = [pl.BlockSpec((H, 8, C), lambda b, s: (0, 0, 0)),
                    pl.BlockSpec((C, QK), lambda b, s: (blk(b, s), 0)),
                    pl.BlockSpec((C, QK), lambda b, s: (blk(b, s), 1)),
                    pl.BlockSpec((C, VT), lambda b, s: (blk(b, s), 1)),
                    pl.BlockSpec((C, LANES), tab),
                    pl.BlockSpec((C, LANES), tab)]
        args = [dr[1 if backward else 0], z, z, z, cos_t, sin_t]
        if yf is not None:
            in_specs += [pl.BlockSpec((C, VT), lambda b, s: (blk(b, s), 0)),
                         pl.BlockSpec((C, VT), lambda b, s: (blk(b, s), 2)),
                         pl.BlockSpec((C, VT), lambda b, s: (blk(b, s), 3))]
            args += [yf, z, z]
        return pl.pallas_call(
            functools.partial(_ret_kernel, H=H, C=C, dk=dk, dv=dv, backward=backward, combine=yf is not None),
            grid=(B, nS + 1),
            in_specs=in_specs,
            out_specs=pl.BlockSpec((C, VT), lambda b, s: (blk(b, s), 0)),
            out_shape=jax.ShapeDtypeStruct((rows.M, VT), BF16),
            scratch_shapes=[pltpu.VMEM((H, dk, dv), F32), pltpu.VMEM((H, C, C), F32),
                            pltpu.VMEM((H, C, LANES), F32), pltpu.VMEM((H, C, LANES), F32),
                            pltpu.VMEM((H, 8, LANES), F32)],
            compiler_params=_cparams("arbitrary", "arbitrary"),
            name="retention_bwd_combine" if backward else "retention_fwd",
        )(*args)

    y_f = run(False, None)
    y = run(True, y_f)
    return _mm_res(y, w_o.astype(BF16), h, mod, rows, layer, 2, rows.M, bm=512)


def _ffn(n, h, w_gu, w_down, mod, rows, layer, m_act, gates=None):
    E, F, D = w_down.shape
    h1 = _swiglu(n, w_gu.astype(BF16), m_act, gates)
    return _mm_res(h1, w_down.reshape(E * F, D).astype(BF16), h, mod, rows, layer, 5, m_act, bk=F // 2)


def kernel(x, c, ctx, c_ctx, mod_w, mod_b, norm_g, final_norm_g, fnet_w_in, fnet_w_out, mla_w_in, mla_q_norm, mla_w_uq, mla_kv_norm, mla_w_ukv, mla_w_o, ret_w_in, ret_decay_rate, ret_w_o, ffn_w_gu, ffn_w_down, moe_router, moe_w_gu, moe_w_down):
    B, S, D = x.shape
    Tc = ctx.shape[1]
    depth = mod_w.shape[0]
    rows = _Rows(B, S, Tc)
    h = jnp.concatenate([x.reshape(rows.ML, D), ctx.reshape(rows.MC, D)], axis=0)
    mod = _adaln(c, c_ctx, mod_w, mod_b)
    for i in range(depth):
        need_ctx = i < depth - 1
        kind, j = i % 3, i // 3
        m_act = rows.M if (need_ctx or kind != 0) else rows.ML
        n = _norm(h, norm_g[i, 0], mod, rows, i, 0, 1, m_act)
        if kind == 0:
            h = _fourier_mixer(n, h, fnet_w_in[j], fnet_w_out[j], mod, rows, i, need_ctx)
        elif kind == 1:
            h = _mla_mixer(n, h, mla_w_in[j], mla_q_norm[j], mla_w_uq[j], mla_kv_norm[j], mla_w_ukv[j],
                           mla_w_o[j], mod, rows, i)
        else:
            h = _retention_mixer(n, h, ret_w_in[j], ret_decay_rate[j], ret_w_o[j], mod, rows, i)
        m_act = rows.M if need_ctx else rows.ML
        m = i // 2
        if i % 2 == 0:
            n = _norm(h, norm_g[i, 1], mod, rows, i, 3, 4, m_act)
            h = _ffn(n, h, ffn_w_gu[m][None], ffn_w_down[m][None], mod, rows, i, m_act)
        else:
            n, gates = _norm(h, norm_g[i, 1], mod, rows, i, 3, 4, m_act, router=moe_router[m])
            h = _ffn(n, h, moe_w_gu[m], moe_w_down[m], mod, rows, i, m_act, gates=gates)
    return _final_norm(h, final_norm_g, rows.ML).reshape(B, S, D)
```

```python
import functools
import math

import numpy as np
import jax
import jax.numpy as jnp
from jax import lax
from jax.experimental import pallas as pl
from jax.experimental.pallas import tpu as pltpu

F32 = jnp.float32
BF16 = jnp.bfloat16

VMEM_LIMIT_BYTES = 56 * 1024 * 1024
LANES = 128

EPS = 1e-6
GN_EPS = 1e-5
ROPE_BASE = 10000.0
GRID_W = 64
MOD_ROWS = 16
FNET_GROUP_DIM = 256
MLA_NOPE = 128
MLA_ROPE = 64
MLA_V = 128
MLA_HEAD_PAD = 256
RET_HEADS = 8
CHUNK = 256
TOP_K = 2


def _cparams(*sem):
    return pltpu.CompilerParams(dimension_semantics=sem, vmem_limit_bytes=VMEM_LIMIT_BYTES)


def _pick(dim, pref):
    if dim <= pref:
        return dim
    t = pref
    while t >= LANES:
        if dim % t == 0:
            return t
        t -= LANES
    return dim


def _dot(a, b):
    return jnp.dot(a, b, preferred_element_type=F32)


def _dot_nt(a, b):
    return lax.dot_general(a, b, (((1,), (1,)), ((), ())), preferred_element_type=F32)


def _dot_tn(a, b):
    return lax.dot_general(a, b, (((0,), (0,)), ((), ())), preferred_element_type=F32)


def _silu(x):
    return x / (1.0 + jnp.exp(-x))


def _adaln_kernel(cond_ref, w_ref, b_ref, o_ref):
    a = _silu(cond_ref[...]).astype(BF16)
    o_ref[0] = _dot(a, w_ref[0].astype(BF16)) + b_ref[0]


def _adaln(c, c_ctx, mod_w, mod_b):
    L, D, D6 = mod_w.shape
    B = c.shape[0]
    assert B + 1 <= MOD_ROWS
    cond = jnp.zeros((MOD_ROWS, D), F32).at[:B].set(c).at[B].set(c_ctx)
    bn = _pick(D6, 1024)
    out = pl.pallas_call(
        _adaln_kernel,
        grid=(L, D6 // bn),
        in_specs=[pl.BlockSpec((MOD_ROWS, D), lambda l, j: (0, 0)),
                  pl.BlockSpec((1, D, bn), lambda l, j: (l, 0, j)),
                  pl.BlockSpec((1, 1, bn), lambda l, j: (l, 0, j))],
        out_specs=pl.BlockSpec((1, MOD_ROWS, bn), lambda l, j: (l, 0, j)),
        out_shape=jax.ShapeDtypeStruct((L, MOD_ROWS, D6), F32),
        compiler_params=_cparams("parallel", "parallel"),
        name="adaln",
    )(cond, mod_w, mod_b.reshape(L, 1, D6))
    return out.reshape(L * MOD_ROWS * 6, 1, D)


class _Rows:
    def __init__(self, B, S, Tc):
        self.B, self.S, self.Tc = B, S, Tc
        self.ML, self.MC = B * S, B * Tc
        self.M = self.ML + self.MC

    def tile(self, pref):
        return _pick(math.gcd(self.S, self.MC), pref)

    def mod_index(self, layer, which, row):
        mi = jnp.where(row >= self.ML, self.B, row // self.S)
        return (layer * MOD_ROWS + mi) * 6 + which


def _norm_mod(h_ref, g_ref, sc_ref, sh_ref):
    x = h_ref[...]
    y = x * lax.rsqrt(jnp.mean(x * x, axis=-1, keepdims=True) + EPS)
    return (y * g_ref[...]) * (1.0 + sc_ref[0]) + sh_ref[0]


def _norm_kernel(h_ref, g_ref, sc_ref, sh_ref, o_ref):
    o_ref[...] = _norm_mod(h_ref, g_ref, sc_ref, sh_ref).astype(o_ref.dtype)


def _norm_router_kernel(h_ref, g_ref, sc_ref, sh_ref, rhi_ref, rlo_ref, o_ref, route_ref, *, n_experts):
    n = _norm_mod(h_ref, g_ref, sc_ref, sh_ref)
    o_ref[...] = n
    nh = n.astype(BF16)
    nl = (n - nh.astype(F32)).astype(BF16)
    lg = _dot(nh, rhi_ref[...]) + (_dot(nl, rhi_ref[...]) + _dot(nh, rlo_ref[...]))
    lane = lax.broadcasted_iota(jnp.int32, lg.shape, 1).astype(F32)
    neg = jnp.float32(-jnp.inf)
    lg = jnp.where(lane < n_experts, lg, neg)
    m1 = jnp.max(lg, axis=-1, keepdims=True)
    i1 = jnp.min(jnp.where(lg == m1, lane, float(LANES)), axis=-1, keepdims=True)
    sel1 = lane == i1
    lg2 = jnp.where(sel1, neg, lg)
    m2 = jnp.max(lg2, axis=-1, keepdims=True)
    i2 = jnp.min(jnp.where(lg2 == m2, lane, float(LANES)), axis=-1, keepdims=True)
    e2 = jnp.exp(m2 - m1)
    w1 = 1.0 / (1.0 + e2)
    w2 = e2 / (1.0 + e2)
    route_ref[...] = jnp.where(lane == 0.0, i1, jnp.where(lane == 1.0, i2,
                               jnp.where(lane == 2.0, w1, jnp.where(lane == 3.0, w2, 0.0))))


def _norm(h, g, mod, rows, layer, sh_idx, sc_idx, m_act, router=None):
    D = h.shape[1]
    bm = rows.tile(512)
    row_spec = pl.BlockSpec((bm, D), lambda i: (i, 0))
    in_specs = [row_spec,
                pl.BlockSpec((1, D), lambda i: (0, 0)),
                pl.BlockSpec((1, 1, D), lambda i: (rows.mod_index(layer, sc_idx, i * bm), 0, 0)),
                pl.BlockSpec((1, 1, D), lambda i: (rows.mod_index(layer, sh_idx, i * bm), 0, 0))]
    args = [h, g.reshape(1, D), mod, mod]
    if router is None:
        return pl.pallas_call(
            _norm_kernel, grid=(m_act // bm,), in_specs=in_specs, out_specs=row_spec,
            out_shape=jax.ShapeDtypeStruct((m_act, D), BF16),
            compiler_params=_cparams("parallel"), name="norm_mod",
        )(*args)
    n_experts = router.shape[1]
    rpad = jnp.zeros((D, LANES), F32).at[:, :n_experts].set(router)
    rhi = rpad.astype(BF16)
    rlo = (rpad - rhi.astype(F32)).astype(BF16)
    full = pl.BlockSpec((D, LANES), lambda i: (0, 0))
    return pl.pallas_call(
        functools.partial(_norm_router_kernel, n_experts=n_experts),
        grid=(m_act // bm,), in_specs=in_specs + [full, full],
        out_specs=[row_spec, pl.BlockSpec((bm, LANES), lambda i: (i, 0))],
        out_shape=[jax.ShapeDtypeStruct((m_act, D), F32), jax.ShapeDtypeStruct((m_act, LANES), F32)],
        compiler_params=_cparams("parallel"), name="norm_mod_router",
    )(*args, rhi, rlo)


def _final_norm_kernel(h_ref, g_ref, o_ref):
    x = h_ref[...]
    y = x * lax.rsqrt(jnp.mean(x * x, axis=-1, keepdims=True) + EPS)
    o_ref[...] = y * g_ref[...]


def _final_norm(h, g, m_act):
    D = h.shape[1]
    bm = _pick(m_act, 512)
    return pl.pallas_call(
        _final_norm_kernel, grid=(m_act // bm,),
        in_specs=[pl.BlockSpec((bm, D), lambda i: (i, 0)), pl.BlockSpec((1, D), lambda i: (0, 0))],
        out_specs=pl.BlockSpec((bm, D), lambda i: (i, 0)),
        out_shape=jax.ShapeDtypeStruct((m_act, D), F32),
        compiler_params=_cparams("parallel"), name="final_norm",
    )(h, g.reshape(1, D))


def _mm_kernel(a_ref, w_ref, o_ref, *acc, nk):
    if nk == 1:
        o_ref[...] = _dot(a_ref[...], w_ref[...]).astype(o_ref.dtype)
        return
    acc_ref, = acc
    k = pl.program_id(2)

    @pl.when(k == 0)
    def _():
        acc_ref[...] = jnp.zeros_like(acc_ref)

    acc_ref[...] += _dot(a_ref[...], w_ref[...])

    @pl.when(k == nk - 1)
    def _():
        o_ref[...] = acc_ref[...].astype(o_ref.dtype)


def _mm(a, w, out_dtype, m_act, bm=1024, bn=1024, bk=None):
    K, N = w.shape
    bm, bn = _pick(m_act, bm), _pick(N, bn)
    bk = K if bk is None else _pick(K, bk)
    nk = K // bk
    return pl.pallas_call(
        functools.partial(_mm_kernel, nk=nk),
        grid=(N // bn, m_act // bm, nk),
        in_specs=[pl.BlockSpec((bm, bk), lambda j, i, k: (i, k)),
                  pl.BlockSpec((bk, bn), lambda j, i, k: (k, j))],
        out_specs=pl.BlockSpec((bm, bn), lambda j, i, k: (i, j)),
        out_shape=jax.ShapeDtypeStruct((m_act, N), out_dtype),
        scratch_shapes=[pltpu.VMEM((bm, bn), F32)] if nk > 1 else [],
        compiler_params=_cparams("parallel", "parallel", "arbitrary"), name="mm",
    )(a, w)


def _mm_res_kernel(a_ref, w_ref, h_ref, gate_ref, o_ref, *acc, nk):
    if nk == 1:
        o_ref[...] = h_ref[...] + gate_ref[0] * _dot(a_ref[...], w_ref[...])
        return
    acc_ref, = acc
    k = pl.program_id(2)

    @pl.when(k == 0)
    def _():
        acc_ref[...] = jnp.zeros_like(acc_ref)

    acc_ref[...] += _dot(a_ref[...], w_ref[...])

    @pl.when(k == nk - 1)
    def _():
        o_ref[...] = h_ref[...] + gate_ref[0] * acc_ref[...]


def _mm_res(a, w, h, mod, rows, layer, gate_idx, m_rows, h_row_off=0, bm=1024, bn=1024, bk=None):
    K, N = w.shape
    bm, bn = _pick(math.gcd(rows.S, rows.MC, m_rows, h_row_off or m_rows), bm), _pick(N, bn)
    bk = K if bk is None else _pick(K, bk)
    nk = K // bk
    off = h_row_off // bm
    return pl.pallas_call(
        functools.partial(_mm_res_kernel, nk=nk),
        grid=(N // bn, m_rows // bm, nk),
        in_specs=[pl.BlockSpec((bm, bk), lambda j, i, k: (i, k)),
                  pl.BlockSpec((bk, bn), lambda j, i, k: (k, j)),
                  pl.BlockSpec((bm, bn), lambda j, i, k: (i + off, j)),
                  pl.BlockSpec((1, 1, bn),
                               lambda j, i, k: (rows.mod_index(layer, gate_idx, (i + off) * bm), 0, j))],
        out_specs=pl.BlockSpec((bm, bn), lambda j, i, k: (i + off, j)),
        out_shape=jax.ShapeDtypeStruct(h.shape, F32),
        scratch_shapes=[pltpu.VMEM((bm, bn), F32)] if nk > 1 else [],
        input_output_aliases={2: 0},
        compiler_params=_cparams("parallel", "parallel", "arbitrary"), name="mm_residual",
    )(a, w, h, mod)


def _swiglu_kernel(a_ref, wg_ref, wu_ref, o_ref):
    a = a_ref[...]
    o_ref[...] = (_silu(_dot(a, wg_ref[...])) * _dot(a, wu_ref[...])).astype(o_ref.dtype)


def _swiglu(n, w_gu, m_act, bm=1024, bn=512):
    D, F2 = w_gu.shape
    F = F2 // 2
    bm, bn = _pick(m_act, bm), _pick(F, bn)
    nf = F // bn
    return pl.pallas_call(
        _swiglu_kernel,
        grid=(nf, m_act // bm),
        in_specs=[pl.BlockSpec((bm, D), lambda j, i: (i, 0)),
                  pl.BlockSpec((D, bn), lambda j, i: (0, j)),
                  pl.BlockSpec((D, bn), lambda j, i: (0, nf + j))],
        out_specs=pl.BlockSpec((bm, bn), lambda j, i: (i, j)),
        out_shape=jax.ShapeDtypeStruct((m_act, F), BF16),
        compiler_params=_cparams("parallel", "parallel"), name="swiglu",
    )(n, w_gu, w_gu)


MOE_TILE = 512
GATHER_ROWS = 256


def _moe_plan(route, n_experts):
    m = route.shape[0]
    P, TM, E = TOP_K * m, MOE_TILE, n_experts
    e_flat = route[:, :TOP_K].astype(jnp.int32).reshape(P)
    w_flat = route[:, TOP_K:2 * TOP_K].reshape(P)
    onehot = (e_flat[:, None] == jnp.arange(E, dtype=jnp.int32)[None, :]).astype(jnp.int32)
    incl = jnp.cumsum(onehot, axis=0)
    rank = jnp.sum((incl - onehot) * onehot, axis=1)
    ptiles = (incl[-1] + TM - 1) // TM
    tile_end = jnp.cumsum(ptiles)
    dest = (tile_end - ptiles)[e_flat] * TM + rank
    n_tiles = P // TM + E
    tile_expert = jnp.minimum(jnp.searchsorted(tile_end, jnp.arange(n_tiles), side="right"), E - 1).astype(jnp.int32)
    tok = jnp.arange(P, dtype=jnp.int32) // TOP_K
    src = jnp.zeros((n_tiles * TM,), jnp.int32).at[dest].set(tok)
    wrow = jnp.zeros((n_tiles * TM,), F32).at[dest].set(w_flat)
    return src, wrow.reshape(-1, 1), dest.astype(jnp.int32), tile_expert, tile_end[-1:].astype(jnp.int32)


def _gather_kernel(idx_ref, src_ref, o_ref, buf_ref, sem, *, G):
    def copy(r):
        return pltpu.make_async_copy(src_ref.at[pl.ds(idx_ref[0, 0, r], 1)], buf_ref.at[pl.ds(r, 1)], sem)

    def start(r, carry):
        copy(r).start()
        return carry

    def wait(r, carry):
        copy(r).wait()
        return carry

    lax.fori_loop(0, G, start, 0, unroll=8)
    lax.fori_loop(0, G, wait, 0, unroll=8)
    o_ref[...] = buf_ref[...].astype(o_ref.dtype)


def _gather_rows(x, idx):
    D = x.shape[1]
    G = GATHER_ROWS
    R = idx.shape[0]
    return pl.pallas_call(
        functools.partial(_gather_kernel, G=G),
        grid=(R // G,),
        in_specs=[pl.BlockSpec((1, 1, G), lambda i: (i, 0, 0), memory_space=pltpu.SMEM),
                  pl.BlockSpec(memory_space=pl.ANY)],
        out_specs=pl.BlockSpec((G, D), lambda i: (i, 0)),
        out_shape=jax.ShapeDtypeStruct((R, D), BF16),
        scratch_shapes=[pltpu.VMEM((G, D), F32), pltpu.SemaphoreType.DMA(())],
        compiler_params=_cparams("arbitrary"), name="moe_dispatch_gather",
    )(idx.reshape(R // G, 1, G), x)


def _moe_swiglu_kernel(te_ref, nu_ref, x_ref, wg_ref, wu_ref, wrow_ref, o_ref):
    active = pl.program_id(1) < nu_ref[0]

    @pl.when(active)
    def _():
        a = x_ref[...]
        r = _silu(_dot(a, wg_ref[0])) * _dot(a, wu_ref[0])
        o_ref[...] = (r * wrow_ref[...]).astype(o_ref.dtype)

    @pl.when(jnp.logical_not(active))
    def _():
        o_ref[...] = jnp.zeros_like(o_ref)


def _moe_down_kernel(te_ref, nu_ref, a_ref, w_ref, o_ref):
    active = pl.program_id(1) < nu_ref[0]

    @pl.when(active)
    def _():
        o_ref[...] = _dot(a_ref[...], w_ref[0])

    @pl.when(jnp.logical_not(active))
    def _():
        o_ref[...] = jnp.zeros_like(o_ref)


def _moe_experts(xs, wrow, tile_expert, n_used, w_gu, w_down, bn=512, bn_down=1024):
    E, D, F2 = w_gu.shape
    F = F2 // 2
    R = xs.shape[0]
    TM = MOE_TILE
    bn = _pick(F, bn)
    nf = F // bn
    h1 = pl.pallas_call(
        _moe_swiglu_kernel,
        grid_spec=pltpu.PrefetchScalarGridSpec(
            num_scalar_prefetch=2, grid=(nf, R // TM),
            in_specs=[pl.BlockSpec((TM, D), lambda j, i, te, nu: (i, 0)),
                      pl.BlockSpec((1, D, bn), lambda j, i, te, nu: (te[i], 0, j)),
                      pl.BlockSpec((1, D, bn), lambda j, i, te, nu: (te[i], 0, nf + j)),
                      pl.BlockSpec((TM, 1), lambda j, i, te, nu: (i, 0))],
            out_specs=pl.BlockSpec((TM, bn), lambda j, i, te, nu: (i, j))),
        out_shape=jax.ShapeDtypeStruct((R, F), BF16),
        compiler_params=_cparams("parallel", "arbitrary"), name="moe_swiglu",
    )(tile_expert, n_used, xs, w_gu, w_gu, wrow)
    bnd = _pick(D, bn_down)
    return pl.pallas_call(
        _moe_down_kernel,
        grid_spec=pltpu.PrefetchScalarGridSpec(
            num_scalar_prefetch=2, grid=(D // bnd, R // TM),
            in_specs=[pl.BlockSpec((TM, F), lambda j, i, te, nu: (i, 0)),
                      pl.BlockSpec((1, F, bnd), lambda j, i, te, nu: (te[i], 0, j))],
            out_specs=pl.BlockSpec((TM, bnd), lambda j, i, te, nu: (i, j))),
        out_shape=jax.ShapeDtypeStruct((R, D), F32),
        compiler_params=_cparams("parallel", "arbitrary"), name="moe_down",
    )(tile_expert, n_used, h1, w_down)


def _combine_kernel(idx_ref, y_ref, h_ref, gate_ref, o_ref, buf_ref, sem, *, G):
    def copy(r, k):
        return pltpu.make_async_copy(y_ref.at[pl.ds(idx_ref[0, 0, TOP_K * r + k], 1)],
                                     buf_ref.at[k, pl.ds(r, 1)], sem.at[k])

    def start(r, carry):
        for k in range(TOP_K):
            copy(r, k).start()
        return carry

    def wait(r, carry):
        for k in range(TOP_K):
            copy(r, k).wait()
        return carry

    lax.fori_loop(0, G, start, 0, unroll=4)
    lax.fori_loop(0, G, wait, 0, unroll=4)
    y = buf_ref[0]
    for k in range(1, TOP_K):
        y = y + buf_ref[k]
    o_ref[...] = h_ref[...] + gate_ref[0] * y


def _moe_combine(y, dest, h, mod, rows, layer, gate_idx, m_act):
    D = h.shape[1]
    G = rows.tile(GATHER_ROWS)
    nt = m_act // G
    return pl.pallas_call(
        functools.partial(_combine_kernel, G=G),
        grid=(nt,),
        in_specs=[pl.BlockSpec((1, 1, TOP_K * G), lambda i: (i, 0, 0), memory_space=pltpu.SMEM),
                  pl.BlockSpec(memory_space=pl.ANY),
                  pl.BlockSpec((G, D), lambda i: (i, 0)),
                  pl.BlockSpec((1, 1, D), lambda i: (rows.mod_index(layer, gate_idx, i * G), 0, 0))],
        out_specs=pl.BlockSpec((G, D), lambda i: (i, 0)),
        out_shape=jax.ShapeDtypeStruct(h.shape, F32),
        scratch_shapes=[pltpu.VMEM((TOP_K, G, D), F32), pltpu.SemaphoreType.DMA((TOP_K,))],
        input_output_aliases={2: 0},
        compiler_params=_cparams("arbitrary"), name="moe_combine",
    )(dest.reshape(nt, 1, TOP_K * G), y, h, mod)


def _dft_cos_sin(n):
    k = np.arange(n, dtype=np.int64)
    ang = 2.0 * np.pi * ((k[:, None] * k[None, :]) % n).astype(np.float64) / n
    return np.cos(ang), np.sin(ang)


def _fnet_chan_kernel(u_ref, w_ref, o_ref, *, groups, gd):
    for g in range(groups):
        r = _dot(u_ref[:, g * gd:(g + 1) * gd], w_ref[...])
        o_ref[0, :, g * gd:(g + 1) * gd] = r[:, :gd].astype(o_ref.dtype)
        o_ref[1, :, g * gd:(g + 1) * gd] = r[:, gd:].astype(o_ref.dtype)


def _fnet_chan(u, m_act):
    D = u.shape[1]
    gd = FNET_GROUP_DIM
    c, s = _dft_cos_sin(gd)
    w = jnp.asarray(np.concatenate([c, -s], axis=1) / math.sqrt(gd), BF16)
    bm = _pick(m_act, 512)
    return pl.pallas_call(
        functools.partial(_fnet_chan_kernel, groups=D // gd, gd=gd),
        grid=(m_act // bm,),
        in_specs=[pl.BlockSpec((bm, D), lambda i: (i, 0)), pl.BlockSpec((gd, 2 * gd), lambda i: (0, 0))],
        out_specs=pl.BlockSpec((2, bm, D), lambda i: (0, i, 0)),
        out_shape=jax.ShapeDtypeStruct((2, m_act, D), BF16),
        compiler_params=_cparams("parallel"), name="fnet_chan_dft",
    )(u, w)


def _fft1_kernel(x_ref, w_ref, tw_ref, o_ref, *, n1, nb, D):
    x2 = x_ref[...].reshape(2 * n1, nb * D)
    y = _dot(w_ref[...], x2)
    yr, yi = y[:n1], y[n1:]
    for t in range(nb):
        twr = tw_ref[0, 0, :, t:t + 1]
        twi = tw_ref[1, 0, :, t:t + 1]
        a, b = yr[:, t * D:(t + 1) * D], yi[:, t * D:(t + 1) * D]
        o_ref[0, 0, t] = (a * twr - b * twi).astype(o_ref.dtype)
        o_ref[0, 1, t] = (a * twi + b * twr).astype(o_ref.dtype)


def _fft3_kernel(z_ref, w_ref, o_ref, *, n2, nb, D):
    for t in range(nb):
        z = z_ref[0, :, :, t * D:(t + 1) * D].reshape(2 * n2, D)
        o_ref[0, :, t * D:(t + 1) * D] = _dot(w_ref[...], z).astype(o_ref.dtype)


def _fnet_seq_latent(p, rows):
    D = p.shape[2]
    B, S = rows.B, rows.S
    n1 = 64
    n2 = S // n1
    assert n1 * n2 == S and n2 % 8 == 0 and rows.MC % n2 == 0
    nb = 8
    c1, s1 = _dft_cos_sin(n1)
    w1 = jnp.asarray(np.block([[c1, s1], [-s1, c1]]) / math.sqrt(n1), BF16)
    f = np.arange(n1)[:, None] * np.arange(n2)[None, :]
    ang = 2.0 * np.pi * f / S
    tw = np.stack([np.cos(ang), -np.sin(ang)]).reshape(2, n1, n2 // nb, nb).transpose(0, 2, 1, 3)
    tw = jnp.asarray(tw, F32)
    pv = p.reshape(2, p.shape[1] // n2, n2 * D)
    y = pl.pallas_call(
        functools.partial(_fft1_kernel, n1=n1, nb=nb, D=D),
        grid=(B, n2 // nb),
        in_specs=[pl.BlockSpec((2, n1, nb * D), lambda b, j: (0, b, j)),
                  pl.BlockSpec((2 * n1, 2 * n1), lambda b, j: (0, 0)),
                  pl.BlockSpec((2, 1, n1, nb), lambda b, j: (0, j, 0, 0))],
        out_specs=pl.BlockSpec((1, 2, nb, n1, D), lambda b, j: (b, 0, j, 0, 0)),
        out_shape=jax.ShapeDtypeStruct((B, 2, n2, n1, D), BF16),
        compiler_params=_cparams("parallel", "parallel"), name="fnet_fft_stage1",
    )(pv, w1, tw)
    c2, s2 = _dft_cos_sin(n2)
    w3 = jnp.asarray(np.concatenate([c2, s2], axis=1) / math.sqrt(n2), BF16)
    yv = y.reshape(B, 2, n2, n1 * D)
    out = pl.pallas_call(
        functools.partial(_fft3_kernel, n2=n2, nb=nb, D=D),
        grid=(B, n1 // nb),
        in_specs=[pl.BlockSpec((1, 2, n2, nb * D), lambda b, j: (b, 0, 0, j)),
                  pl.BlockSpec((n2, 2 * n2), lambda b, j: (0, 0))],
        out_specs=pl.BlockSpec((1, n2, nb * D), lambda b, j: (b, 0, j)),
        out_shape=jax.ShapeDtypeStruct((B, n2, n1 * D), BF16),
        compiler_params=_cparams("parallel", "parallel"), name="fnet_fft_stage2",
    )(yv, w3)
    return out.reshape(B * S, D)


def _dft_ctx_kernel(z_ref, w_ref, o_ref, *, T, D):
    z = z_ref[...].reshape(2 * T, D)
    o_ref[...] = _dot(w_ref[...], z).astype(o_ref.dtype)


def _fnet_seq_ctx(p, rows):
    D = p.shape[2]
    T = rows.Tc
    c, s = _dft_cos_sin(T)
    w = jnp.asarray(np.concatenate([c, s], axis=1) / math.sqrt(T), BF16)
    off = rows.ML // T
    return pl.pallas_call(
        functools.partial(_dft_ctx_kernel, T=T, D=D),
        grid=(rows.B,),
        in_specs=[pl.BlockSpec((2, T, D), lambda b: (0, off + b, 0)),
                  pl.BlockSpec((T, 2 * T), lambda b: (0, 0))],
        out_specs=pl.BlockSpec((T, D), lambda b: (b, 0)),
        out_shape=jax.ShapeDtypeStruct((rows.MC, D), BF16),
        compiler_params=_cparams("parallel"), name="fnet_dft_ctx",
    )(p, w)


def _fourier_mixer(n, h, w_in, w_out, mod, rows, layer, need_ctx):
    m_act = rows.M if need_ctx else rows.ML
    u = _mm(n, w_in.astype(BF16), BF16, m_act)
    p = _fnet_chan(u, m_act)
    f_lat = _fnet_seq_latent(p, rows)
    w_out = w_out.astype(BF16)
    h = _mm_res(f_lat, w_out, h, mod, rows, layer, 2, rows.ML)
    if need_ctx:
        f_ctx = _fnet_seq_ctx(p, rows)
        h = _mm_res(f_ctx, w_out, h, mod, rows, layer, 2, rows.MC, h_row_off=rows.ML)
    return h


def _mla_proj_kernel(z_ref, qn_ref, kvn_ref, wq_ref, wqs_ref, wkv_ref, cos_ref, sin_ref,
                     q_ref, k_ref, v_ref, *, H, qr, kvr, scale):
    def rms(x, g):
        return ((x * lax.rsqrt(jnp.mean(x * x, axis=-1, keepdims=True) + EPS)) * g).astype(BF16)

    qn = rms(z_ref[:, :qr], qn_ref[...])
    kvn = rms(z_ref[:, qr:qr + kvr], kvn_ref[...])
    kr = z_ref[:, qr + kvr:qr + kvr + LANES]
    kr_sw = z_ref[:, qr + kvr + LANES:qr + kvr + 2 * LANES]
    cos, sin = cos_ref[...], sin_ref[...]
    k_rope = (kr * cos + kr_sw * sin).astype(BF16)
    q = _dot(qn, wq_ref[...]) * scale
    q_sw = _dot(qn, wqs_ref[...]) * scale
    kv = _dot(kvn, wkv_ref[...])
    P = MLA_HEAD_PAD
    for hh in range(H):
        q_ref[0, hh, :, :LANES] = q[:, hh * P:hh * P + LANES].astype(BF16)
        q_ref[0, hh, :, LANES:] = (q[:, hh * P + LANES:(hh + 1) * P] * cos
                                   + q_sw[:, hh * LANES:(hh + 1) * LANES] * sin).astype(BF16)
        k_ref[0, hh, :, :LANES] = kv[:, hh * LANES:(hh + 1) * LANES].astype(BF16)
        k_ref[0, hh, :, LANES:] = k_rope
        v_ref[0, hh] = kv[:, (H + hh) * LANES:(H + hh + 1) * LANES].astype(BF16)


def _mla_attn_kernel(q_ref, k_ref, v_ref, o_ref, *, S, n_lat_blocks):
    def attend(k, v):
        s = _dot_nt(q_ref[0, 0], k)
        m = jnp.max(s, axis=-1, keepdims=True)
        p = jnp.exp(s - m)
        l = jnp.sum(p, axis=-1, keepdims=True)
        o_ref[...] = (_dot(p.astype(BF16), v) / l).astype(o_ref.dtype)

    i = pl.program_id(2)

    @pl.when(i < n_lat_blocks)
    def _():
        attend(k_ref[0, 0], v_ref[0, 0])

    @pl.when(i >= n_lat_blocks)
    def _():
        attend(k_ref[0, 0, S:, :], v_ref[0, 0, S:, :])


def _axial_rope_table(S, Tc):
    half = MLA_ROPE // 2
    rows_ = S // GRID_W
    row = jnp.repeat(jnp.arange(rows_, dtype=F32), GRID_W)
    col = jnp.tile(jnp.arange(GRID_W, dtype=F32), rows_)
    n_freq = MLA_ROPE // 4
    inv = ROPE_BASE ** (-jnp.arange(n_freq, dtype=F32) / n_freq)
    ang = jnp.concatenate([row[:, None] * inv, col[:, None] * inv], axis=-1)
    cos, sin = jnp.cos(ang), jnp.sin(ang)
    zpad = jnp.zeros((S, LANES - 2 * half), F32)
    cos_l = jnp.concatenate([cos, cos, zpad], axis=-1)
    sin_l = jnp.concatenate([sin, sin, zpad], axis=-1)
    cos_c = jnp.concatenate([jnp.ones((Tc, 2 * half), F32), jnp.zeros((Tc, LANES - 2 * half), F32)], axis=-1)
    return jnp.concatenate([cos_l, cos_c], axis=0), jnp.concatenate([sin_l, jnp.zeros((Tc, LANES), F32)], axis=0)


def _swap_halves_neg(w):
    half = w.shape[-1] // 2
    return jnp.concatenate([-w[..., half:], w[..., :half]], axis=-1)


def _mla_mixer(n, h, w_in, q_norm, w_uq, kv_norm, w_ukv, w_o, mod, rows, layer):
    B, S, Tc = rows.B, rows.S, rows.Tc
    D = n.shape[1]
    qr, kvr = q_norm.shape[0], kv_norm.shape[0]
    H = w_uq.shape[1] // (MLA_NOPE + MLA_ROPE)
    C = CHUNK
    assert Tc == C and S % C == 0 and MLA_NOPE == LANES and MLA_V == LANES
    pad = LANES - MLA_ROPE
    w_kr = w_in[:, qr + kvr:]
    zc = jnp.zeros((D, pad), F32)
    w_in_p = jnp.concatenate([w_in[:, :qr + kvr], w_kr, zc, _swap_halves_neg(w_kr), zc], axis=1).astype(BF16)
    wq = w_uq.reshape(qr, H, MLA_NOPE + MLA_ROPE)
    zq = jnp.zeros((qr, H, pad), F32)
    wq_p = jnp.concatenate([wq, zq], axis=-1).reshape(qr, H * MLA_HEAD_PAD).astype(BF16)
    wq_sw = jnp.concatenate([_swap_halves_neg(wq[..., MLA_NOPE:]), zq], axis=-1).reshape(qr, H * LANES).astype(BF16)
    wkv = w_ukv.reshape(kvr, H, MLA_NOPE + MLA_V)
    wkv_p = jnp.concatenate([wkv[..., :MLA_NOPE].reshape(kvr, H * LANES),
                             wkv[..., MLA_NOPE:].reshape(kvr, H * LANES)], axis=1).astype(BF16)
    cos_t, sin_t = _axial_rope_table(S, Tc)

    z = _mm(n, w_in_p, F32, rows.M)
    nS = S // C
    n_lat_tiles = rows.ML // C
    T = S + Tc

    def bt(i):
        lat = i < n_lat_tiles
        return jnp.where(lat, i // nS, i - n_lat_tiles), jnp.where(lat, i % nS, nS)

    def kv_map(i):
        b, t = bt(i)
        return (b, 0, t, 0)

    ZW = w_in_p.shape[1]
    const = lambda shape: pl.BlockSpec(shape, lambda i: (0,) * len(shape))
    q, k, v = pl.pallas_call(
        functools.partial(_mla_proj_kernel, H=H, qr=qr, kvr=kvr,
                          scale=float((MLA_NOPE + MLA_ROPE) ** -0.5)),
        grid=(rows.M // C,),
        in_specs=[pl.BlockSpec((C, ZW), lambda i: (i, 0)),
                  const((1, qr)), const((1, kvr)),
                  const(wq_p.shape), const(wq_sw.shape), const(wkv_p.shape),
                  pl.BlockSpec((C, LANES), lambda i: (bt(i)[1], 0)),
                  pl.BlockSpec((C, LANES), lambda i: (bt(i)[1], 0))],
        out_specs=[pl.BlockSpec((1, H, C, MLA_HEAD_PAD), kv_map),
                   pl.BlockSpec((1, H, C, MLA_HEAD_PAD), kv_map),
                   pl.BlockSpec((1, H, C, MLA_V), kv_map)],
        out_shape=[jax.ShapeDtypeStruct((B, H, T, MLA_HEAD_PAD), BF16),
                   jax.ShapeDtypeStruct((B, H, T, MLA_HEAD_PAD), BF16),
                   jax.ShapeDtypeStruct((B, H, T, MLA_V), BF16)],
        compiler_params=_cparams("parallel"), name="mla_project",
    )(z, q_norm.reshape(1, qr), kv_norm.reshape(1, kvr), wq_p, wq_sw, wkv_p, cos_t, sin_t)

    bq = C
    o = pl.pallas_call(
        functools.partial(_mla_attn_kernel, S=S, n_lat_blocks=nS),
        grid=(B, H, nS + 1),
        in_specs=[pl.BlockSpec((1, 1, bq, MLA_HEAD_PAD), lambda b, hh, i: (b, hh, i, 0)),
                  pl.BlockSpec((1, 1, T, MLA_HEAD_PAD), lambda b, hh, i: (b, hh, 0, 0)),
                  pl.BlockSpec((1, 1, T, MLA_V), lambda b, hh, i: (b, hh, 0, 0))],
        out_specs=pl.BlockSpec((bq, MLA_V),
                               lambda b, hh, i: (jnp.where(i < nS, b * nS + i, n_lat_tiles + b), hh)),
        out_shape=jax.ShapeDtypeStruct((rows.M, H * MLA_V), BF16),
        compiler_params=_cparams("parallel", "parallel", "arbitrary"), name="mla_attention",
    )(q, k, v)
    return _mm_res(o, w_o.astype(BF16), h, mod, rows, layer, 2, rows.M)


def _ret_kernel(dr_ref, q_ref, k_ref, v_ref, cos_ref, sin_ref, *rest, H, C, dk, dv, backward, combine):
    if combine:
        yf_ref, gf_ref, gb_ref, o_ref, R_ref, D_ref, qd_ref, kd_ref, cd_ref = rest
    else:
        o_ref, R_ref, D_ref, qd_ref, kd_ref, cd_ref = rest
    b, s = pl.program_id(0), pl.program_id(1)
    half = dk // 2

    @pl.when((b == 0) & (s == 0))
    def _():
        ri = lax.broadcasted_iota(jnp.int32, (C, C), 0).astype(F32)
        ci = lax.broadcasted_iota(jnp.int32, (C, C), 1).astype(F32)
        diff = (ci - ri) if backward else (ri - ci)
        r = lax.broadcasted_iota(jnp.int32, (C, LANES), 0).astype(F32)
        qpow = (C - r) if backward else (r + 1.0)
        kpow = r if backward else (C - 1.0 - r)
        for hh in range(H):
            lg = jnp.log1p(-jnp.exp(dr_ref[hh]))
            D_ref[hh] = jnp.where(diff >= 0, jnp.exp(jnp.maximum(diff, 0.0) * lg[0:1, :]), 0.0)
            qd_ref[hh] = jnp.exp(qpow * lg[0:1, :LANES])
            kd_ref[hh] = jnp.exp(kpow * lg[0:1, :LANES])
            cd_ref[hh] = jnp.exp(float(C) * lg[:, :LANES])

    @pl.when(s == 0)
    def _():
        R_ref[...] = jnp.zeros_like(R_ref)

    cos, sin = cos_ref[...], sin_ref[...]

    def rope(x):
        x1, x2 = x[:, :half], x[:, half:]
        return jnp.concatenate([x1 * cos - x2 * sin, x1 * sin + x2 * cos], axis=-1)

    for hh in range(H):
        q = rope(q_ref[:, hh * dk:(hh + 1) * dk].astype(F32))
        k = rope(k_ref[:, hh * dk:(hh + 1) * dk].astype(F32)) * (dk ** -0.5)
        v = v_ref[:, hh * dv:(hh + 1) * dv]
        qd = jnp.concatenate([qd_ref[hh]] * (dk // LANES), axis=-1)
        kd = jnp.concatenate([kd_ref[hh]] * (dk // LANES), axis=-1)
        cd = jnp.concatenate([cd_ref[hh][0:1]] * (dv // LANES), axis=-1)
        a = _dot_nt(q.astype(BF16), k.astype(BF16)) * D_ref[hh]
        R = R_ref[hh]
        o = _dot(a.astype(BF16), v) + _dot((q * qd).astype(BF16), R.astype(BF16))
        R_ref[hh] = R * cd + _dot_tn((k * kd).astype(BF16), v)
        mu = jnp.mean(o, axis=-1, keepdims=True)
        d = o - mu
        y = d * lax.rsqrt(jnp.mean(d * d, axis=-1, keepdims=True) + GN_EPS)
        sl = slice(hh * dv, (hh + 1) * dv)
        if combine:
            y = (_silu(gf_ref[:, sl].astype(F32)) * yf_ref[:, sl].astype(F32)
                 + _silu(gb_ref[:, sl].astype(F32)) * y)
        o_ref[:, sl] = y.astype(o_ref.dtype)


def _retention_rope_table(S, Tc, dim):
    inv = ROPE_BASE ** (-jnp.linspace(0.0, 1.0, dim // 2, dtype=F32))
    ang = jnp.arange(S, dtype=F32)[:, None] * inv
    cos = jnp.concatenate([jnp.cos(ang), jnp.ones((Tc, dim // 2), F32)], axis=0)
    sin = jnp.concatenate([jnp.sin(ang), jnp.zeros((Tc, dim // 2), F32)], axis=0)
    return cos, sin


def _retention_mixer(n, h, w_in, decay_rate, w_o, mod, rows, layer):
    B, S, Tc = rows.B, rows.S, rows.Tc
    D = n.shape[1]
    H = RET_HEADS
    dk = D // H
    dv = 2 * dk
    QK, VT = H * dk, H * dv
    C = CHUNK
    assert Tc == C and S % C == 0 and dk // 2 == LANES and w_in.shape[1] == 2 * QK + 3 * VT and VT == 2 * QK
    z = _mm(n, w_in.astype(BF16), BF16, rows.M)
    cos_t, sin_t = _retention_rope_table(S, Tc, dk)
    nS = S // C
    n_lat_tiles = rows.ML // C
    dr = jnp.broadcast_to(decay_rate.astype(F32)[:, :, None, None], (2, H, 8, C))

    def run(backward, yf):
        def blk(b, s):
            j = (nS - s) if backward else (s - 1)
            return jnp.where(s == 0, n_lat_tiles + b, b * nS + j)

        def tab(b, s):
            j = (nS - s) if backward else (s - 1)
            return (jnp.where(s == 0, nS, j), 0)

        in_specs = [pl.BlockSpec((H, 8, C), lambda b, s: (0, 0, 0)),
                    pl.BlockSpec((C, QK), lambda b, s: (blk(b, s), 0)),
                    pl.BlockSpec((C, QK), lambda b, s: (blk(b, s), 1)),
                    pl.BlockSpec((C, VT), lambda b, s: (blk(b, s), 1)),
                    pl.BlockSpec((C, LANES), tab),
                    pl.BlockSpec((C, LANES), tab)]
        args = [dr[1 if backward else 0], z, z, z, cos_t, sin_t]
        if yf is not None:
            in_specs += [pl.BlockSpec((C, VT), lambda b, s: (blk(b, s), 0)),
                         pl.BlockSpec((C, VT), lambda b, s: (blk(b, s), 2)),
                         pl.BlockSpec((C, VT), lambda b, s: (blk(b, s), 3))]
            args += [yf, z, z]
        return pl.pallas_call(
            functools.partial(_ret_kernel, H=H, C=C, dk=dk, dv=dv, backward=backward, combine=yf is not None),
            grid=(B, nS + 1),
            in_specs=in_specs,
            out_specs=pl.BlockSpec((C, VT), lambda b, s: (blk(b, s), 0)),
            out_shape=jax.ShapeDtypeStruct((rows.M, VT), BF16),
            scratch_shapes=[pltpu.VMEM((H, dk, dv), F32), pltpu.VMEM((H, C, C), F32),
                            pltpu.VMEM((H, C, LANES), F32), pltpu.VMEM((H, C, LANES), F32),
                            pltpu.VMEM((H, 8, LANES), F32)],
            compiler_params=_cparams("arbitrary", "arbitrary"),
            name="retention_bwd_combine" if backward else "retention_fwd",
        )(*args)

    y_f = run(False, None)
    y = run(True, y_f)
    return _mm_res(y, w_o.astype(BF16), h, mod, rows, layer, 2, rows.M, bm=512)


def _ffn(n, h, w_gu, w_down, mod, rows, layer, m_act):
    h1 = _swiglu(n, w_gu.astype(BF16), m_act)
    return _mm_res(h1, w_down.astype(BF16), h, mod, rows, layer, 5, m_act, bk=w_down.shape[0] // 2)


def _moe_ffn(n, route, h, w_gu, w_down, mod, rows, layer, m_act):
    src, wrow, dest, tile_expert, n_used = _moe_plan(route, w_gu.shape[0])
    xs = _gather_rows(n, src)
    y = _moe_experts(xs, wrow, tile_expert, n_used, w_gu.astype(BF16), w_down.astype(BF16))
    return _moe_combine(y, dest, h, mod, rows, layer, 5, m_act)


def kernel(x, c, ctx, c_ctx, mod_w, mod_b, norm_g, final_norm_g, fnet_w_in, fnet_w_out, mla_w_in, mla_q_norm, mla_w_uq, mla_kv_norm, mla_w_ukv, mla_w_o, ret_w_in, ret_decay_rate, ret_w_o, ffn_w_gu, ffn_w_down, moe_router, moe_w_gu, moe_w_down):
    B, S, D = x.shape
    Tc = ctx.shape[1]
    depth = mod_w.shape[0]
    rows = _Rows(B, S, Tc)
    h = jnp.concatenate([x.reshape(rows.ML, D), ctx.reshape(rows.MC, D)], axis=0)
    mod = _adaln(c, c_ctx, mod_w, mod_b)
    for i in range(depth):
        need_ctx = i < depth - 1
        kind, j = i % 3, i // 3
        m_act = rows.M if (need_ctx or kind != 0) else rows.ML
        n = _norm(h, norm_g[i, 0], mod, rows, i, 0, 1, m_act)
        if kind == 0:
            h = _fourier_mixer(n, h, fnet_w_in[j], fnet_w_out[j], mod, rows, i, need_ctx)
        elif kind == 1:
            h = _mla_mixer(n, h, mla_w_in[j], mla_q_norm[j], mla_w_uq[j], mla_kv_norm[j], mla_w_ukv[j],
                           mla_w_o[j], mod, rows, i)
        else:
            h = _retention_mixer(n, h, ret_w_in[j], ret_decay_rate[j], ret_w_o[j], mod, rows, i)
        m_act = rows.M if need_ctx else rows.ML
        m = i // 2
        if i % 2 == 0:
            n = _norm(h, norm_g[i, 1], mod, rows, i, 3, 4, m_act)
            h = _ffn(n, h, ffn_w_gu[m], ffn_w_down[m], mod, rows, i, m_act)
        else:
            n, route = _norm(h, norm_g[i, 1], mod, rows, i, 3, 4, m_act, router=moe_router[m])
            h = _moe_ffn(n, route, h, moe_w_gu[m], moe_w_down[m], mod, rows, i, m_act)
    return _final_norm(h, final_norm_g, rows.ML).reshape(B, S, D)
```

```python
import functools
import math

import numpy as np
import jax
import jax.numpy as jnp
from jax import lax
from jax.experimental import pallas as pl
from jax.experimental.pallas import tpu as pltpu

F32 = jnp.float32
BF16 = jnp.bfloat16

VMEM_LIMIT_BYTES = 56 * 1024 * 1024
LANES = 128

EPS = 1e-6
GN_EPS = 1e-5
ROPE_BASE = 10000.0
GRID_W = 64
MOD_ROWS = 16
FNET_GROUP_DIM = 256
MLA_NOPE = 128
MLA_ROPE = 64
MLA_V = 128
MLA_HEAD_PAD = 256
RET_HEADS = 8
CHUNK = 256
TOP_K = 2


def _cparams(*sem):
    return pltpu.CompilerParams(dimension_semantics=sem, vmem_limit_bytes=VMEM_LIMIT_BYTES)


def _pick(dim, pref):
    if dim <= pref:
        return dim
    t = pref
    while t >= LANES:
        if dim % t == 0:
            return t
        t -= LANES
    return dim


def _dot(a, b):
    return jnp.dot(a, b, preferred_element_type=F32)


def _dot_nt(a, b):
    return lax.dot_general(a, b, (((1,), (1,)), ((), ())), preferred_element_type=F32)


def _dot_tn(a, b):
    return lax.dot_general(a, b, (((0,), (0,)), ((), ())), preferred_element_type=F32)


def _silu(x):
    return x / (1.0 + jnp.exp(-x))


def _adaln_kernel(cond_ref, w_ref, b_ref, o_ref):
    a = _silu(cond_ref[...]).astype(BF16)
    o_ref[0] = _dot(a, w_ref[0].astype(BF16)) + b_ref[0]


def _adaln(c, c_ctx, mod_w, mod_b):
    L, D, D6 = mod_w.shape
    B = c.shape[0]
    assert B + 1 <= MOD_ROWS
    cond = jnp.zeros((MOD_ROWS, D), F32).at[:B].set(c).at[B].set(c_ctx)
    bn = _pick(D6, 1024)
    out = pl.pallas_call(
        _adaln_kernel,
        grid=(L, D6 // bn),
        in_specs=[pl.BlockSpec((MOD_ROWS, D), lambda l, j: (0, 0)),
                  pl.BlockSpec((1, D, bn), lambda l, j: (l, 0, j)),
                  pl.BlockSpec((1, 1, bn), lambda l, j: (l, 0, j))],
        out_specs=pl.BlockSpec((1, MOD_ROWS, bn), lambda l, j: (l, 0, j)),
        out_shape=jax.ShapeDtypeStruct((L, MOD_ROWS, D6), F32),
        compiler_params=_cparams("parallel", "parallel"),
        name="adaln",
    )(cond, mod_w, mod_b.reshape(L, 1, D6))
    return out.reshape(L * MOD_ROWS * 6, 1, D)


class _Rows:
    def __init__(self, B, S, Tc):
        self.B, self.S, self.Tc = B, S, Tc
        self.ML, self.MC = B * S, B * Tc
        self.M = self.ML + self.MC

    def tile(self, pref):
        return _pick(math.gcd(self.S, self.MC), pref)

    def mod_index(self, layer, which, row):
        mi = jnp.where(row >= self.ML, self.B, row // self.S)
        return (layer * MOD_ROWS + mi) * 6 + which


def _norm_mod(h_ref, g_ref, sc_ref, sh_ref):
    x = h_ref[...]
    y = x * lax.rsqrt(jnp.mean(x * x, axis=-1, keepdims=True) + EPS)
    return (y * g_ref[...]) * (1.0 + sc_ref[0]) + sh_ref[0]


def _norm_kernel(h_ref, g_ref, sc_ref, sh_ref, o_ref):
    o_ref[...] = _norm_mod(h_ref, g_ref, sc_ref, sh_ref).astype(o_ref.dtype)


def _norm_router_kernel(h_ref, g_ref, sc_ref, sh_ref, rhi_ref, rlo_ref, o_ref, route_ref, *, n_experts):
    n = _norm_mod(h_ref, g_ref, sc_ref, sh_ref)
    o_ref[...] = n
    nh = n.astype(BF16)
    nl = (n - nh.astype(F32)).astype(BF16)
    lg = _dot(nh, rhi_ref[...]) + (_dot(nl, rhi_ref[...]) + _dot(nh, rlo_ref[...]))
    lane = lax.broadcasted_iota(jnp.int32, lg.shape, 1).astype(F32)
    neg = jnp.float32(-jnp.inf)
    lg = jnp.where(lane < n_experts, lg, neg)
    m1 = jnp.max(lg, axis=-1, keepdims=True)
    i1 = jnp.min(jnp.where(lg == m1, lane, float(LANES)), axis=-1, keepdims=True)
    sel1 = lane == i1
    lg2 = jnp.where(sel1, neg, lg)
    m2 = jnp.max(lg2, axis=-1, keepdims=True)
    i2 = jnp.min(jnp.where(lg2 == m2, lane, float(LANES)), axis=-1, keepdims=True)
    e2 = jnp.exp(m2 - m1)
    w1 = 1.0 / (1.0 + e2)
    w2 = e2 / (1.0 + e2)
    route_ref[...] = jnp.where(lane == 0.0, i1, jnp.where(lane == 1.0, i2,
                               jnp.where(lane == 2.0, w1, jnp.where(lane == 3.0, w2, 0.0))))


def _norm(h, g, mod, rows, layer, sh_idx, sc_idx, m_act, router=None):
    D = h.shape[1]
    bm = rows.tile(512)
    row_spec = pl.BlockSpec((bm, D), lambda i: (i, 0))
    in_specs = [row_spec,
                pl.BlockSpec((1, D), lambda i: (0, 0)),
                pl.BlockSpec((1, 1, D), lambda i: (rows.mod_index(layer, sc_idx, i * bm), 0, 0)),
                pl.BlockSpec((1, 1, D), lambda i: (rows.mod_index(layer, sh_idx, i * bm), 0, 0))]
    args = [h, g.reshape(1, D), mod, mod]
    if router is None:
        return pl.pallas_call(
            _norm_kernel, grid=(m_act // bm,), in_specs=in_specs, out_specs=row_spec,
            out_shape=jax.ShapeDtypeStruct((m_act, D), BF16),
            compiler_params=_cparams("parallel"), name="norm_mod",
        )(*args)
    n_experts = router.shape[1]
    rpad = jnp.zeros((D, LANES), F32).at[:, :n_experts].set(router)
    rhi = rpad.astype(BF16)
    rlo = (rpad - rhi.astype(F32)).astype(BF16)
    full = pl.BlockSpec((D, LANES), lambda i: (0, 0))
    return pl.pallas_call(
        functools.partial(_norm_router_kernel, n_experts=n_experts),
        grid=(m_act // bm,), in_specs=in_specs + [full, full],
        out_specs=[row_spec, pl.BlockSpec((bm, LANES), lambda i: (i, 0))],
        out_shape=[jax.ShapeDtypeStruct((m_act, D), F32), jax.ShapeDtypeStruct((m_act, LANES), F32)],
        compiler_params=_cparams("parallel"), name="norm_mod_router",
    )(*args, rhi, rlo)


def _final_norm_kernel(h_ref, g_ref, o_ref):
    x = h_ref[...]
    y = x * lax.rsqrt(jnp.mean(x * x, axis=-1, keepdims=True) + EPS)
    o_ref[...] = y * g_ref[...]


def _final_norm(h, g, m_act):
    D = h.shape[1]
    bm = _pick(m_act, 512)
    return pl.pallas_call(
        _final_norm_kernel, grid=(m_act // bm,),
        in_specs=[pl.BlockSpec((bm, D), lambda i: (i, 0)), pl.BlockSpec((1, D), lambda i: (0, 0))],
        out_specs=pl.BlockSpec((bm, D), lambda i: (i, 0)),
        out_shape=jax.ShapeDtypeStruct((m_act, D), F32),
        compiler_params=_cparams("parallel"), name="final_norm",
    )(h, g.reshape(1, D))


def _mm_kernel(a_ref, w_ref, o_ref, *acc, nk):
    if nk == 1:
        o_ref[...] = _dot(a_ref[...], w_ref[...]).astype(o_ref.dtype)
        return
    acc_ref, = acc
    k = pl.program_id(2)

    @pl.when(k == 0)
    def _():
        acc_ref[...] = jnp.zeros_like(acc_ref)

    acc_ref[...] += _dot(a_ref[...], w_ref[...])

    @pl.when(k == nk - 1)
    def _():
        o_ref[...] = acc_ref[...].astype(o_ref.dtype)


def _stacked(w):
    return w if isinstance(w, tuple) else (w[None], 0)


def _mm(a, w, out_dtype, m_act, bm=1024, bn=1024, bk=None):
    w, wl = _stacked(w)
    _, K, N = w.shape
    bm, bn = _pick(m_act, bm), _pick(N, bn)
    bk = K if bk is None else _pick(K, bk)
    nk = K // bk
    return pl.pallas_call(
        functools.partial(_mm_kernel, nk=nk),
        grid=(N // bn, m_act // bm, nk),
        in_specs=[pl.BlockSpec((bm, bk), lambda j, i, k: (i, k)),
                  pl.BlockSpec((None, bk, bn), lambda j, i, k: (wl, k, j))],
        out_specs=pl.BlockSpec((bm, bn), lambda j, i, k: (i, j)),
        out_shape=jax.ShapeDtypeStruct((m_act, N), out_dtype),
        scratch_shapes=[pltpu.VMEM((bm, bn), F32)] if nk > 1 else [],
        compiler_params=_cparams("parallel", "parallel", "arbitrary"), name="mm",
    )(a, w)


def _mm_res_kernel(a_ref, w_ref, h_ref, gate_ref, o_ref, *acc, nk):
    if nk == 1:
        o_ref[...] = h_ref[...] + gate_ref[0] * _dot(a_ref[...], w_ref[...])
        return
    acc_ref, = acc
    k = pl.program_id(2)

    @pl.when(k == 0)
    def _():
        acc_ref[...] = jnp.zeros_like(acc_ref)

    acc_ref[...] += _dot(a_ref[...], w_ref[...])

    @pl.when(k == nk - 1)
    def _():
        o_ref[...] = h_ref[...] + gate_ref[0] * acc_ref[...]


def _mm_res(a, w, h, mod, rows, layer, gate_idx, m_rows, h_row_off=0, bm=1024, bn=1024, bk=None):
    w, wl = _stacked(w)
    _, K, N = w.shape
    bm, bn = _pick(math.gcd(rows.S, rows.MC, m_rows, h_row_off or m_rows), bm), _pick(N, bn)
    bk = K if bk is None else _pick(K, bk)
    nk = K // bk
    off = h_row_off // bm
    return pl.pallas_call(
        functools.partial(_mm_res_kernel, nk=nk),
        grid=(N // bn, m_rows // bm, nk),
        in_specs=[pl.BlockSpec((bm, bk), lambda j, i, k: (i, k)),
                  pl.BlockSpec((None, bk, bn), lambda j, i, k: (wl, k, j)),
                  pl.BlockSpec((bm, bn), lambda j, i, k: (i + off, j)),
                  pl.BlockSpec((1, 1, bn),
                               lambda j, i, k: (rows.mod_index(layer, gate_idx, (i + off) * bm), 0, j))],
        out_specs=pl.BlockSpec((bm, bn), lambda j, i, k: (i + off, j)),
        out_shape=jax.ShapeDtypeStruct(h.shape, F32),
        scratch_shapes=[pltpu.VMEM((bm, bn), F32)] if nk > 1 else [],
        input_output_aliases={2: 0},
        compiler_params=_cparams("parallel", "parallel", "arbitrary"), name="mm_residual",
    )(a, w, h, mod)


def _swiglu_kernel(a_ref, wg_ref, wu_ref, o_ref):
    a = a_ref[...]
    o_ref[...] = (_silu(_dot(a, wg_ref[...])) * _dot(a, wu_ref[...])).astype(o_ref.dtype)


def _swiglu(n, w_gu, m_act, bm=1024, bn=512):
    w_gu, wl = _stacked(w_gu)
    _, D, F2 = w_gu.shape
    F = F2 // 2
    bm, bn = _pick(m_act, bm), _pick(F, bn)
    nf = F // bn
    return pl.pallas_call(
        _swiglu_kernel,
        grid=(nf, m_act // bm),
        in_specs=[pl.BlockSpec((bm, D), lambda j, i: (i, 0)),
                  pl.BlockSpec((None, D, bn), lambda j, i: (wl, 0, j)),
                  pl.BlockSpec((None, D, bn), lambda j, i: (wl, 0, nf + j))],
        out_specs=pl.BlockSpec((bm, bn), lambda j, i: (i, j)),
        out_shape=jax.ShapeDtypeStruct((m_act, F), BF16),
        compiler_params=_cparams("parallel", "parallel"), name="swiglu",
    )(n, w_gu, w_gu)


MOE_TILE = 512
GATHER_ROWS = 256


def _moe_plan(route, n_experts):
    m = route.shape[0]
    P, TM, E = TOP_K * m, MOE_TILE, n_experts
    e_flat = route[:, :TOP_K].astype(jnp.int32).reshape(P)
    onehot = (e_flat[:, None] == jnp.arange(E, dtype=jnp.int32)[None, :]).astype(jnp.int32)
    incl = jnp.cumsum(onehot, axis=0)
    rank = jnp.sum((incl - onehot) * onehot, axis=1)
    ptiles = (incl[-1] + TM - 1) // TM
    tile_end = jnp.cumsum(ptiles)
    dest = (tile_end - ptiles)[e_flat] * TM + rank
    n_tiles = P // TM + E
    tile_expert = jnp.minimum(jnp.searchsorted(tile_end, jnp.arange(n_tiles), side="right"), E - 1).astype(jnp.int32)
    tok = jnp.arange(P, dtype=jnp.int32) // TOP_K
    src = jnp.zeros((n_tiles * TM,), jnp.int32).at[dest].set(tok)
    return src, dest.astype(jnp.int32), tile_expert, tile_end[-1:].astype(jnp.int32)


def _row_gather_pipeline(issue, wait, n_steps):
    i = pl.program_id(0)
    slot = i % 2

    @pl.when(i == 0)
    def _():
        issue(False, 0)

    @pl.when(i + 1 < n_steps)
    def _():
        issue(True, 1 - slot)

    wait(slot)
    return slot


def _gather_kernel(idx_ref, nxt_ref, src_ref, o_ref, buf_ref, sem, *, G, n_steps):
    def copy(ref, r, s):
        return pltpu.make_async_copy(src_ref.at[pl.ds(ref[0, 0, r], 1)], buf_ref.at[s, pl.ds(r, 1)], sem.at[s])

    def issue(ahead, s):
        ref = nxt_ref if ahead else idx_ref

        def body(r, carry):
            copy(ref, r, s).start()
            return carry

        lax.fori_loop(0, G, body, 0, unroll=8)

    def wait(s):
        def body(r, carry):
            copy(idx_ref, r, s).wait()
            return carry

        lax.fori_loop(0, G, body, 0, unroll=8)

    slot = _row_gather_pipeline(issue, wait, n_steps)
    o_ref[...] = buf_ref[slot].astype(o_ref.dtype)


def _gather_rows(x, idx):
    D = x.shape[1]
    G = GATHER_ROWS
    R = idx.shape[0]
    n = R // G
    smem = lambda f: pl.BlockSpec((1, 1, G), f, memory_space=pltpu.SMEM)
    return pl.pallas_call(
        functools.partial(_gather_kernel, G=G, n_steps=n),
        grid=(n,),
        in_specs=[smem(lambda i: (i, 0, 0)), smem(lambda i: (jnp.minimum(i + 1, n - 1), 0, 0)),
                  pl.BlockSpec(memory_space=pl.ANY)],
        out_specs=pl.BlockSpec((G, D), lambda i: (i, 0)),
        out_shape=jax.ShapeDtypeStruct((R, D), BF16),
        scratch_shapes=[pltpu.VMEM((2, G, D), F32), pltpu.SemaphoreType.DMA((2,))],
        compiler_params=_cparams("arbitrary"), name="moe_dispatch_gather",
    )(idx.reshape(n, 1, G), idx.reshape(n, 1, G), x)


def _moe_swiglu_kernel(te_ref, nu_ref, x_ref, wg_ref, wu_ref, o_ref, wgb_ref, wub_ref):
    i = pl.program_id(1)
    active = i < nu_ref[0]

    @pl.when(jnp.logical_or(i == 0, te_ref[i] != te_ref[jnp.maximum(i - 1, 0)]))
    def _():
        wgb_ref[...] = wg_ref[...].astype(BF16)
        wub_ref[...] = wu_ref[...].astype(BF16)

    @pl.when(active)
    def _():
        a = x_ref[...]
        o_ref[...] = (_silu(_dot(a, wgb_ref[...])) * _dot(a, wub_ref[...])).astype(o_ref.dtype)

    @pl.when(jnp.logical_not(active))
    def _():
        o_ref[...] = jnp.zeros_like(o_ref)


def _moe_down_kernel(te_ref, nu_ref, a_ref, w_ref, o_ref):
    active = pl.program_id(1) < nu_ref[0]

    @pl.when(active)
    def _():
        o_ref[...] = _dot(a_ref[...], w_ref[...])

    @pl.when(jnp.logical_not(active))
    def _():
        o_ref[...] = jnp.zeros_like(o_ref)


def _moe_experts(xs, tile_expert, n_used, w_gu, w_down, m, bn=512, bn_down=1024):
    _, E, D, F2 = w_gu.shape
    F = F2 // 2
    R = xs.shape[0]
    TM = MOE_TILE
    bn = _pick(F, bn)
    nf = F // bn
    h1 = pl.pallas_call(
        _moe_swiglu_kernel,
        grid_spec=pltpu.PrefetchScalarGridSpec(
            num_scalar_prefetch=2, grid=(nf, R // TM),
            in_specs=[pl.BlockSpec((TM, D), lambda j, i, te, nu: (i, 0)),
                      pl.BlockSpec((None, None, D, bn), lambda j, i, te, nu: (m, te[i], 0, j)),
                      pl.BlockSpec((None, None, D, bn), lambda j, i, te, nu: (m, te[i], 0, nf + j))],
            out_specs=pl.BlockSpec((TM, bn), lambda j, i, te, nu: (i, j)),
            scratch_shapes=[pltpu.VMEM((D, bn), BF16), pltpu.VMEM((D, bn), BF16)]),
        out_shape=jax.ShapeDtypeStruct((R, F), BF16),
        compiler_params=_cparams("arbitrary", "arbitrary"), name="moe_swiglu",
    )(tile_expert, n_used, xs, w_gu, w_gu)
    bnd = _pick(D, bn_down)
    return pl.pallas_call(
        _moe_down_kernel,
        grid_spec=pltpu.PrefetchScalarGridSpec(
            num_scalar_prefetch=2, grid=(D // bnd, R // TM),
            in_specs=[pl.BlockSpec((TM, F), lambda j, i, te, nu: (i, 0)),
                      pl.BlockSpec((None, None, F, bnd), lambda j, i, te, nu: (m, te[i], 0, j))],
            out_specs=pl.BlockSpec((TM, bnd), lambda j, i, te, nu: (i, j))),
        out_shape=jax.ShapeDtypeStruct((R, D), F32),
        compiler_params=_cparams("parallel", "arbitrary"), name="moe_down",
    )(tile_expert, n_used, h1, w_down)


def _combine_kernel(idx_ref, nxt_ref, y_ref, route_ref, h_ref, gate_ref, o_ref, buf_ref, sem, *, G, n_steps):
    def copy(ref, r, k, s):
        return pltpu.make_async_copy(y_ref.at[pl.ds(ref[0, 0, TOP_K * r + k], 1)],
                                     buf_ref.at[s, k, pl.ds(r, 1)], sem.at[s])

    def issue(ahead, s):
        ref = nxt_ref if ahead else idx_ref

        def body(r, carry):
            for k in range(TOP_K):
                copy(ref, r, k, s).start()
            return carry

        lax.fori_loop(0, G, body, 0, unroll=4)

    def wait(s):
        def body(r, carry):
            for k in range(TOP_K):
                copy(idx_ref, r, k, s).wait()
            return carry

        lax.fori_loop(0, G, body, 0, unroll=4)

    slot = _row_gather_pipeline(issue, wait, n_steps)
    y = route_ref[:, TOP_K:TOP_K + 1] * buf_ref[slot, 0]
    for k in range(1, TOP_K):
        y = y + route_ref[:, TOP_K + k:TOP_K + k + 1] * buf_ref[slot, k]
    o_ref[...] = h_ref[...] + gate_ref[0] * y


def _moe_combine(y, dest, route, h, mod, rows, layer, gate_idx, m_act):
    D = h.shape[1]
    G = rows.tile(GATHER_ROWS)
    n = m_act // G
    smem = lambda f: pl.BlockSpec((1, 1, TOP_K * G), f, memory_space=pltpu.SMEM)
    idx = dest.reshape(n, 1, TOP_K * G)
    return pl.pallas_call(
        functools.partial(_combine_kernel, G=G, n_steps=n),
        grid=(n,),
        in_specs=[smem(lambda i: (i, 0, 0)), smem(lambda i: (jnp.minimum(i + 1, n - 1), 0, 0)),
                  pl.BlockSpec(memory_space=pl.ANY),
                  pl.BlockSpec((G, LANES), lambda i: (i, 0)),
                  pl.BlockSpec((G, D), lambda i: (i, 0)),
                  pl.BlockSpec((1, 1, D), lambda i: (rows.mod_index(layer, gate_idx, i * G), 0, 0))],
        out_specs=pl.BlockSpec((G, D), lambda i: (i, 0)),
        out_shape=jax.ShapeDtypeStruct(h.shape, F32),
        scratch_shapes=[pltpu.VMEM((2, TOP_K, G, D), F32), pltpu.SemaphoreType.DMA((2,))],
        input_output_aliases={4: 0},
        compiler_params=_cparams("arbitrary"), name="moe_combine",
    )(idx, idx, y, route, h, mod)


def _dft_cos_sin(n):
    k = np.arange(n, dtype=np.int64)
    ang = 2.0 * np.pi * ((k[:, None] * k[None, :]) % n).astype(np.float64) / n
    return np.cos(ang), np.sin(ang)


def _fnet_chan_kernel(u_ref, w_ref, o_ref, *, groups, gd):
    for g in range(groups):
        r = _dot(u_ref[:, g * gd:(g + 1) * gd], w_ref[...])
        o_ref[0, :, g * gd:(g + 1) * gd] = r[:, :gd].astype(o_ref.dtype)
        o_ref[1, :, g * gd:(g + 1) * gd] = r[:, gd:].astype(o_ref.dtype)


def _fnet_chan(u, m_act):
    D = u.shape[1]
    gd = FNET_GROUP_DIM
    c, s = _dft_cos_sin(gd)
    w = jnp.asarray(np.concatenate([c, -s], axis=1) / math.sqrt(gd), BF16)
    bm = _pick(m_act, 512)
    return pl.pallas_call(
        functools.partial(_fnet_chan_kernel, groups=D // gd, gd=gd),
        grid=(m_act // bm,),
        in_specs=[pl.BlockSpec((bm, D), lambda i: (i, 0)), pl.BlockSpec((gd, 2 * gd), lambda i: (0, 0))],
        out_specs=pl.BlockSpec((2, bm, D), lambda i: (0, i, 0)),
        out_shape=jax.ShapeDtypeStruct((2, m_act, D), BF16),
        compiler_params=_cparams("parallel"), name="fnet_chan_dft",
    )(u, w)


def _fft1_kernel(x_ref, w_ref, tw_ref, o_ref, *, n1, nb, D):
    x2 = x_ref[...].reshape(2 * n1, nb * D)
    y = _dot(w_ref[...], x2)
    yr, yi = y[:n1], y[n1:]
    for t in range(nb):
        twr = tw_ref[0, 0, :, t:t + 1]
        twi = tw_ref[1, 0, :, t:t + 1]
        a, b = yr[:, t * D:(t + 1) * D], yi[:, t * D:(t + 1) * D]
        o_ref[0, 0, t] = (a * twr - b * twi).astype(o_ref.dtype)
        o_ref[0, 1, t] = (a * twi + b * twr).astype(o_ref.dtype)


def _fft3_kernel(z_ref, w_ref, o_ref, *, n2, nb, D):
    for t in range(nb):
        z = z_ref[0, :, :, t * D:(t + 1) * D].reshape(2 * n2, D)
        o_ref[0, :, t * D:(t + 1) * D] = _dot(w_ref[...], z).astype(o_ref.dtype)


def _fnet_seq_latent(p, rows):
    D = p.shape[2]
    B, S = rows.B, rows.S
    n1 = 64
    n2 = S // n1
    assert n1 * n2 == S and n2 % 8 == 0 and rows.MC % n2 == 0
    nb = 8
    c1, s1 = _dft_cos_sin(n1)
    w1 = jnp.asarray(np.block([[c1, s1], [-s1, c1]]) / math.sqrt(n1), BF16)
    f = np.arange(n1)[:, None] * np.arange(n2)[None, :]
    ang = 2.0 * np.pi * f / S
    tw = np.stack([np.cos(ang), -np.sin(ang)]).reshape(2, n1, n2 // nb, nb).transpose(0, 2, 1, 3)
    tw = jnp.asarray(tw, F32)
    pv = p.reshape(2, p.shape[1] // n2, n2 * D)
    y = pl.pallas_call(
        functools.partial(_fft1_kernel, n1=n1, nb=nb, D=D),
        grid=(B, n2 // nb),
        in_specs=[pl.BlockSpec((2, n1, nb * D), lambda b, j: (0, b, j)),
                  pl.BlockSpec((2 * n1, 2 * n1), lambda b, j: (0, 0)),
                  pl.BlockSpec((2, 1, n1, nb), lambda b, j: (0, j, 0, 0))],
        out_specs=pl.BlockSpec((1, 2, nb, n1, D), lambda b, j: (b, 0, j, 0, 0)),
        out_shape=jax.ShapeDtypeStruct((B, 2, n2, n1, D), BF16),
        compiler_params=_cparams("parallel", "parallel"), name="fnet_fft_stage1",
    )(pv, w1, tw)
    c2, s2 = _dft_cos_sin(n2)
    w3 = jnp.asarray(np.concatenate([c2, s2], axis=1) / math.sqrt(n2), BF16)
    yv = y.reshape(B, 2, n2, n1 * D)
    out = pl.pallas_call(
        functools.partial(_fft3_kernel, n2=n2, nb=nb, D=D),
        grid=(B, n1 // nb),
        in_specs=[pl.BlockSpec((1, 2, n2, nb * D), lambda b, j: (b, 0, 0, j)),
                  pl.BlockSpec((n2, 2 * n2), lambda b, j: (0, 0))],
        out_specs=pl.BlockSpec((1, n2, nb * D), lambda b, j: (b, 0, j)),
        out_shape=jax.ShapeDtypeStruct((B, n2, n1 * D), BF16),
        compiler_params=_cparams("parallel", "parallel"), name="fnet_fft_stage2",
    )(yv, w3)
    return out.reshape(B * S, D)


def _dft_ctx_kernel(z_ref, w_ref, o_ref, *, T, D):
    z = z_ref[...].reshape(2 * T, D)
    o_ref[...] = _dot(w_ref[...], z).astype(o_ref.dtype)


def _fnet_seq_ctx(p, rows):
    D = p.shape[2]
    T = rows.Tc
    c, s = _dft_cos_sin(T)
    w = jnp.asarray(np.concatenate([c, s], axis=1) / math.sqrt(T), BF16)
    off = rows.ML // T
    return pl.pallas_call(
        functools.partial(_dft_ctx_kernel, T=T, D=D),
        grid=(rows.B,),
        in_specs=[pl.BlockSpec((2, T, D), lambda b: (0, off + b, 0)),
                  pl.BlockSpec((T, 2 * T), lambda b: (0, 0))],
        out_specs=pl.BlockSpec((T, D), lambda b: (b, 0)),
        out_shape=jax.ShapeDtypeStruct((rows.MC, D), BF16),
        compiler_params=_cparams("parallel"), name="fnet_dft_ctx",
    )(p, w)


def _fourier_mixer(n, h, w_in, w_out, mod, rows, layer, need_ctx):
    m_act = rows.M if need_ctx else rows.ML
    u = _mm(n, w_in, BF16, m_act)
    p = _fnet_chan(u, m_act)
    f_lat = _fnet_seq_latent(p, rows)
    h = _mm_res(f_lat, w_out, h, mod, rows, layer, 2, rows.ML)
    if need_ctx:
        f_ctx = _fnet_seq_ctx(p, rows)
        h = _mm_res(f_ctx, w_out, h, mod, rows, layer, 2, rows.MC, h_row_off=rows.ML)
    return h


def _mla_proj_kernel(z_ref, qn_ref, kvn_ref, wq_ref, wqs_ref, wkv_ref, cos_ref, sin_ref,
                     q_ref, k_ref, v_ref, *, H, qr, kvr, scale):
    def rms(x, g):
        return ((x * lax.rsqrt(jnp.mean(x * x, axis=-1, keepdims=True) + EPS)) * g).astype(BF16)

    qn = rms(z_ref[:, :qr], qn_ref[...])
    kvn = rms(z_ref[:, qr:qr + kvr], kvn_ref[...])
    kr = z_ref[:, qr + kvr:qr + kvr + LANES]
    kr_sw = z_ref[:, qr + kvr + LANES:qr + kvr + 2 * LANES]
    cos, sin = cos_ref[...], sin_ref[...]
    k_rope = (kr * cos + kr_sw * sin).astype(BF16)
    q = _dot(qn, wq_ref[...]) * scale
    q_sw = _dot(qn, wqs_ref[...]) * scale
    kv = _dot(kvn, wkv_ref[...])
    P = MLA_HEAD_PAD
    for hh in range(H):
        q_ref[0, hh, :, :LANES] = q[:, hh * P:hh * P + LANES].astype(BF16)
        q_ref[0, hh, :, LANES:] = (q[:, hh * P + LANES:(hh + 1) * P] * cos
                                   + q_sw[:, hh * LANES:(hh + 1) * LANES] * sin).astype(BF16)
        k_ref[0, hh, :, :LANES] = kv[:, hh * LANES:(hh + 1) * LANES].astype(BF16)
        k_ref[0, hh, :, LANES:] = k_rope
        v_ref[0, hh] = kv[:, (H + hh) * LANES:(H + hh + 1) * LANES].T.astype(BF16)


def _mla_attn_kernel(q_ref, k_ref, vt_ref, o_ref):
    s_t = _dot_nt(k_ref[0, 0], q_ref[0, 0])
    m = jnp.max(s_t, axis=0, keepdims=True)
    p = jnp.exp(s_t - m)
    l = jnp.sum(p, axis=0, keepdims=True)
    o_t = _dot(vt_ref[0, 0], p.astype(BF16)) / l
    o_ref[...] = o_t.T.astype(o_ref.dtype)


def _axial_rope_table(S, Tc):
    half = MLA_ROPE // 2
    rows_ = S // GRID_W
    row = jnp.repeat(jnp.arange(rows_, dtype=F32), GRID_W)
    col = jnp.tile(jnp.arange(GRID_W, dtype=F32), rows_)
    n_freq = MLA_ROPE // 4
    inv = ROPE_BASE ** (-jnp.arange(n_freq, dtype=F32) / n_freq)
    ang = jnp.concatenate([row[:, None] * inv, col[:, None] * inv], axis=-1)
    cos, sin = jnp.cos(ang), jnp.sin(ang)
    zpad = jnp.zeros((S, LANES - 2 * half), F32)
    cos_l = jnp.concatenate([cos, cos, zpad], axis=-1)
    sin_l = jnp.concatenate([sin, sin, zpad], axis=-1)
    cos_c = jnp.concatenate([jnp.ones((Tc, 2 * half), F32), jnp.zeros((Tc, LANES - 2 * half), F32)], axis=-1)
    return jnp.concatenate([cos_l, cos_c], axis=0), jnp.concatenate([sin_l, jnp.zeros((Tc, LANES), F32)], axis=0)


def _swap_halves_neg(w):
    half = w.shape[-1] // 2
    return jnp.concatenate([-w[..., half:], w[..., :half]], axis=-1)


def _mla_mixer(n, h, w_in, q_norm, w_uq, kv_norm, w_ukv, w_o, mod, rows, layer):
    B, S, Tc = rows.B, rows.S, rows.Tc
    D = n.shape[1]
    qr, kvr = q_norm.shape[0], kv_norm.shape[0]
    H = w_uq.shape[1] // (MLA_NOPE + MLA_ROPE)
    C = CHUNK
    assert Tc == C and S % C == 0 and MLA_NOPE == LANES and MLA_V == LANES
    pad = LANES - MLA_ROPE
    w_kr = w_in[:, qr + kvr:]
    zc = jnp.zeros((D, pad), F32)
    w_in_p = jnp.concatenate([w_in[:, :qr + kvr], w_kr, zc, _swap_halves_neg(w_kr), zc], axis=1).astype(BF16)
    wq = w_uq.reshape(qr, H, MLA_NOPE + MLA_ROPE)
    zq = jnp.zeros((qr, H, pad), F32)
    wq_p = jnp.concatenate([wq, zq], axis=-1).reshape(qr, H * MLA_HEAD_PAD).astype(BF16)
    wq_sw = jnp.concatenate([_swap_halves_neg(wq[..., MLA_NOPE:]), zq], axis=-1).reshape(qr, H * LANES).astype(BF16)
    wkv = w_ukv.reshape(kvr, H, MLA_NOPE + MLA_V)
    wkv_p = jnp.concatenate([wkv[..., :MLA_NOPE].reshape(kvr, H * LANES),
                             wkv[..., MLA_NOPE:].reshape(kvr, H * LANES)], axis=1).astype(BF16)
    cos_t, sin_t = _axial_rope_table(S, Tc)

    z = _mm(n, w_in_p, F32, rows.M)
    nS = S // C
    n_lat_tiles = rows.ML // C
    T = S + Tc

    def bt(i):
        lat = i < n_lat_tiles
        return jnp.where(lat, i // nS, i - n_lat_tiles), jnp.where(lat, i % nS, nS)

    def kv_map(i):
        b, t = bt(i)
        return (b, 0, t, 0)

    ZW = w_in_p.shape[1]
    const = lambda shape: pl.BlockSpec(shape, lambda i: (0,) * len(shape))
    q, k, v_t = pl.pallas_call(
        functools.partial(_mla_proj_kernel, H=H, qr=qr, kvr=kvr,
                          scale=float((MLA_NOPE + MLA_ROPE) ** -0.5)),
        grid=(rows.M // C,),
        in_specs=[pl.BlockSpec((C, ZW), lambda i: (i, 0)),
                  const((1, qr)), const((1, kvr)),
                  const(wq_p.shape), const(wq_sw.shape), const(wkv_p.shape),
                  pl.BlockSpec((C, LANES), lambda i: (bt(i)[1], 0)),
                  pl.BlockSpec((C, LANES), lambda i: (bt(i)[1], 0))],
        out_specs=[pl.BlockSpec((1, H, C, MLA_HEAD_PAD), kv_map),
                   pl.BlockSpec((1, H, C, MLA_HEAD_PAD), kv_map),
                   pl.BlockSpec((1, H, MLA_V, C), lambda i: (bt(i)[0], 0, 0, bt(i)[1]))],
        out_shape=[jax.ShapeDtypeStruct((B, H, T, MLA_HEAD_PAD), BF16),
                   jax.ShapeDtypeStruct((B, H, T, MLA_HEAD_PAD), BF16),
                   jax.ShapeDtypeStruct((B, H, MLA_V, T), BF16)],
        compiler_params=_cparams("parallel"), name="mla_project",
    )(z, q_norm.reshape(1, qr), kv_norm.reshape(1, kvr), wq_p, wq_sw, wkv_p, cos_t, sin_t)

    bq = _pick(S, 512)
    nq = S // bq
    o_lat = pl.pallas_call(
        _mla_attn_kernel,
        grid=(B, H, nq),
        in_specs=[pl.BlockSpec((1, 1, bq, MLA_HEAD_PAD), lambda b, hh, i: (b, hh, i, 0)),
                  pl.BlockSpec((1, 1, T, MLA_HEAD_PAD), lambda b, hh, i: (b, hh, 0, 0)),
                  pl.BlockSpec((1, 1, MLA_V, T), lambda b, hh, i: (b, hh, 0, 0))],
        out_specs=pl.BlockSpec((bq, MLA_V), lambda b, hh, i: (b * nq + i, hh)),
        out_shape=jax.ShapeDtypeStruct((rows.ML, H * MLA_V), BF16),
        compiler_params=_cparams("parallel", "parallel", "arbitrary"), name="mla_attention",
    )(q, k, v_t)
    o_ctx = pl.pallas_call(
        _mla_attn_kernel,
        grid=(B, H),
        in_specs=[pl.BlockSpec((1, 1, Tc, MLA_HEAD_PAD), lambda b, hh: (b, hh, nS, 0)),
                  pl.BlockSpec((1, 1, Tc, MLA_HEAD_PAD), lambda b, hh: (b, hh, nS, 0)),
                  pl.BlockSpec((1, 1, MLA_V, Tc), lambda b, hh: (b, hh, 0, nS))],
        out_specs=pl.BlockSpec((Tc, MLA_V), lambda b, hh: (b, hh)),
        out_shape=jax.ShapeDtypeStruct((rows.MC, H * MLA_V), BF16),
        compiler_params=_cparams("parallel", "parallel"), name="mla_attention_ctx",
    )(q, k, v_t)
    h = _mm_res(o_lat, w_o, h, mod, rows, layer, 2, rows.ML)
    return _mm_res(o_ctx, w_o, h, mod, rows, layer, 2, rows.MC, h_row_off=rows.ML)


def _ret_kernel(dr_ref, q_ref, k_ref, v_ref, cos_ref, sin_ref, *rest, H, C, dk, dv, backward, combine):
    if combine:
        yf_ref, gf_ref, gb_ref, o_ref, R_ref, D_ref, qd_ref, kd_ref, cd_ref = rest
    else:
        o_ref, R_ref, D_ref, qd_ref, kd_ref, cd_ref = rest
    b, s = pl.program_id(0), pl.program_id(1)
    half = dk // 2

    @pl.when((b == 0) & (s == 0))
    def _():
        ri = lax.broadcasted_iota(jnp.int32, (C, C), 0).astype(F32)
        ci = lax.broadcasted_iota(jnp.int32, (C, C), 1).astype(F32)
        diff = (ci - ri) if backward else (ri - ci)
        r = lax.broadcasted_iota(jnp.int32, (C, LANES), 0).astype(F32)
        qpow = (C - r) if backward else (r + 1.0)
        kpow = r if backward else (C - 1.0 - r)
        for hh in range(H):
            lg = jnp.log1p(-jnp.exp(dr_ref[hh]))
            D_ref[hh] = jnp.where(diff >= 0, jnp.exp(jnp.maximum(diff, 0.0) * lg[0:1, :]), 0.0)
            qd_ref[hh] = jnp.exp(qpow * lg[0:1, :LANES])
            kd_ref[hh] = jnp.exp(kpow * lg[0:1, :LANES])
            cd_ref[hh] = jnp.exp(float(C) * lg[:, :LANES])

    @pl.when(s == 0)
    def _():
        R_ref[...] = jnp.zeros_like(R_ref)

    cos, sin = cos_ref[...], sin_ref[...]

    def rope(x):
        x1, x2 = x[:, :half], x[:, half:]
        return jnp.concatenate([x1 * cos - x2 * sin, x1 * sin + x2 * cos], axis=-1)

    for hh in range(H):
        q = rope(q_ref[:, hh * dk:(hh + 1) * dk].astype(F32))
        k = rope(k_ref[:, hh * dk:(hh + 1) * dk].astype(F32)) * (dk ** -0.5)
        v = v_ref[:, hh * dv:(hh + 1) * dv]
        qd = jnp.concatenate([qd_ref[hh]] * (dk // LANES), axis=-1)
        kd = jnp.concatenate([kd_ref[hh]] * (dk // LANES), axis=-1)
        cd = jnp.concatenate([cd_ref[hh][0:1]] * (dv // LANES), axis=-1)
        a = _dot_nt(q.astype(BF16), k.astype(BF16)) * D_ref[hh]
        R = R_ref[hh]
        o = _dot(a.astype(BF16), v) + _dot((q * qd).astype(BF16), R.astype(BF16))
        R_ref[hh] = R * cd + _dot_tn((k * kd).astype(BF16), v)
        mu = jnp.mean(o, axis=-1, keepdims=True)
        d = o - mu
        y = d * lax.rsqrt(jnp.mean(d * d, axis=-1, keepdims=True) + GN_EPS)
        sl = slice(hh * dv, (hh + 1) * dv)
        if combine:
            y = (_silu(gf_ref[:, sl].astype(F32)) * yf_ref[:, sl].astype(F32)
                 + _silu(gb_ref[:, sl].astype(F32)) * y)
        o_ref[:, sl] = y.astype(o_ref.dtype)


def _retention_rope_table(S, Tc, dim):
    inv = ROPE_BASE ** (-jnp.linspace(0.0, 1.0, dim // 2, dtype=F32))
    ang = jnp.arange(S, dtype=F32)[:, None] * inv
    cos = jnp.concatenate([jnp.cos(ang), jnp.ones((Tc, dim // 2), F32)], axis=0)
    sin = jnp.concatenate([jnp.sin(ang), jnp.zeros((Tc, dim // 2), F32)], axis=0)
    return cos, sin


def _retention_mixer(n, h, w_in, decay_rate, w_o, mod, rows, layer):
    B, S, Tc = rows.B, rows.S, rows.Tc
    D = n.shape[1]
    H = RET_HEADS
    dk = D // H
    dv = 2 * dk
    QK, VT = H * dk, H * dv
    C = CHUNK
    assert Tc == C and S % C == 0 and dk // 2 == LANES and w_in[0].shape[2] == 2 * QK + 3 * VT and VT == 2 * QK
    z = _mm(n, w_in, BF16, rows.M)
    cos_t, sin_t = _retention_rope_table(S, Tc, dk)
    nS = S // C
    n_lat_tiles = rows.ML // C
    dr = jnp.broadcast_to(decay_rate.astype(F32)[:, :, None, None], (2, H, 8, C))

    def run(backward, yf):
        def blk(b, s):
            j = (nS - s) if backward else (s - 1)
            return jnp.where(s == 0, n_lat_tiles + b, b * nS + j)

        def tab(b, s):
            j = (nS - s) if backward else (s - 1)
            return (jnp.where(s == 0, nS, j), 0)

        in_specs = [pl.BlockSpec((H, 8, C), lambda b, s: (0, 0, 0)),
                    pl.BlockSpec((C, QK), lambda b, s: (blk(b, s), 0)),
                    pl.BlockSpec((C, QK), lambda b, s: (blk(b, s), 1)),
                    pl.BlockSpec((C, VT), lambda b, s: (blk(b, s), 1)),
                    pl.BlockSpec((C, LANES), tab),
                    pl.BlockSpec((C, LANES), tab)]
        args = [dr[1 if backward else 0], z, z, z, cos_t, sin_t]
        if yf is not None:
            in_specs += [pl.BlockSpec((C, VT), lambda b, s: (blk(b, s), 0)),
                         pl.BlockSpec((C, VT), lambda b, s: (blk(b, s), 2)),
                         pl.BlockSpec((C, VT), lambda b, s: (blk(b, s), 3))]
            args += [yf, z, z]
        return pl.pallas_call(
            functools.partial(_ret_kernel, H=H, C=C, dk=dk, dv=dv, backward=backward, combine=yf is not None),
            grid=(B, nS + 1),
            in_specs=in_specs,
            out_specs=pl.BlockSpec((C, VT), lambda b, s: (blk(b, s), 0)),
            out_shape=jax.ShapeDtypeStruct((rows.M, VT), BF16),
            scratch_shapes=[pltpu.VMEM((H, dk, dv), F32), pltpu.VMEM((H, C, C), F32),
                            pltpu.VMEM((H, C, LANES), F32), pltpu.VMEM((H, C, LANES), F32),
                            pltpu.VMEM((H, 8, LANES), F32)],
            compiler_params=_cparams("arbitrary", "arbitrary"),
            name="retention_bwd_combine" if backward else "retention_fwd",
        )(*args)

    y_f = run(False, None)
    y = run(True, y_f)
    return _mm_res(y, w_o, h, mod, rows, layer, 2, rows.M, bm=512)


def _ffn(n, h, w_gu, w_down, mod, rows, layer, m_act):
    h1 = _swiglu(n, w_gu, m_act)
    return _mm_res(h1, w_down, h, mod, rows, layer, 5, m_act, bk=w_down[0].shape[1] // 2)


def _moe_ffn(n, route, h, w_gu, w_down, m, mod, rows, layer, m_act):
    src, dest, tile_expert, n_used = _moe_plan(route, w_gu.shape[1])
    xs = _gather_rows(n, src)
    y = _moe_experts(xs, tile_expert, n_used, w_gu, w_down, m)
    return _moe_combine(y, dest, route, h, mod, rows, layer, 5, m_act)


def kernel(x, c, ctx, c_ctx, mod_w, mod_b, norm_g, final_norm_g, fnet_w_in, fnet_w_out, mla_w_in, mla_q_norm, mla_w_uq, mla_kv_norm, mla_w_ukv, mla_w_o, ret_w_in, ret_decay_rate, ret_w_o, ffn_w_gu, ffn_w_down, moe_router, moe_w_gu, moe_w_down):
    B, S, D = x.shape
    Tc = ctx.shape[1]
    depth = mod_w.shape[0]
    rows = _Rows(B, S, Tc)
    h = jnp.concatenate([x.reshape(rows.ML, D), ctx.reshape(rows.MC, D)], axis=0)
    mod = _adaln(c, c_ctx, mod_w, mod_b)
    fnet_w_in, fnet_w_out, ret_w_in, ret_w_o, mla_w_o, ffn_w_gu, ffn_w_down, moe_w_down = (
        w.astype(BF16) for w in (fnet_w_in, fnet_w_out, ret_w_in, ret_w_o, mla_w_o, ffn_w_gu, ffn_w_down,
                                 moe_w_down))
    for i in range(depth):
        need_ctx = i < depth - 1
        kind, j = i % 3, i // 3
        m_act = rows.M if (need_ctx or kind != 0) else rows.ML
        n = _norm(h, norm_g[i, 0], mod, rows, i, 0, 1, m_act)
        if kind == 0:
            h = _fourier_mixer(n, h, (fnet_w_in, j), (fnet_w_out, j), mod, rows, i, need_ctx)
        elif kind == 1:
            h = _mla_mixer(n, h, mla_w_in[j], mla_q_norm[j], mla_w_uq[j], mla_kv_norm[j], mla_w_ukv[j],
                           (mla_w_o, j), mod, rows, i)
        else:
            h = _retention_mixer(n, h, (ret_w_in, j), ret_decay_rate[j], (ret_w_o, j), mod, rows, i)
        m_act = rows.M if need_ctx else rows.ML
        m = i // 2
        if i % 2 == 0:
            n = _norm(h, norm_g[i, 1], mod, rows, i, 3, 4, m_act)
            h = _ffn(n, h, (ffn_w_gu, m), (ffn_w_down, m), mod, rows, i, m_act)
        else:
            n, route = _norm(h, norm_g[i, 1], mod, rows, i, 3, 4, m_act, router=moe_router[m])
            h = _moe_ffn(n, route, h, moe_w_gu, moe_w_down, m, mod, rows, i, m_act)
    return _final_norm(h, final_norm_g, rows.ML).reshape(B, S, D)
```

```python
import functools
import math

import numpy as np
import jax
import jax.numpy as jnp
from jax import lax
from jax.experimental import pallas as pl
from jax.experimental.pallas import tpu as pltpu

F32 = jnp.float32
BF16 = jnp.bfloat16

VMEM_LIMIT_BYTES = 56 * 1024 * 1024
LANES = 128

EPS = 1e-6
GN_EPS = 1e-5
ROPE_BASE = 10000.0
GRID_W = 64
MOD_ROWS = 16
FNET_GROUP_DIM = 256
MLA_NOPE = 128
MLA_ROPE = 64
MLA_V = 128
MLA_HEAD_PAD = 256
RET_HEADS = 8
CHUNK = 256
TOP_K = 2


def _cparams(*sem):
    return pltpu.CompilerParams(dimension_semantics=sem, vmem_limit_bytes=VMEM_LIMIT_BYTES)


def _pick(dim, pref):
    if dim <= pref:
        return dim
    t = pref
    while t >= LANES:
        if dim % t == 0:
            return t
        t -= LANES
    return dim


def _dot(a, b):
    return jnp.dot(a, b, preferred_element_type=F32)


def _dot_nt(a, b):
    return lax.dot_general(a, b, (((1,), (1,)), ((), ())), preferred_element_type=F32)


def _dot_tn(a, b):
    return lax.dot_general(a, b, (((0,), (0,)), ((), ())), preferred_element_type=F32)


def _silu(x):
    return x / (1.0 + jnp.exp(-x))


def _adaln_kernel(cond_ref, w_ref, b_ref, o_ref):
    a = _silu(cond_ref[...]).astype(BF16)
    o_ref[0] = _dot(a, w_ref[0].astype(BF16)) + b_ref[0]


def _adaln(c, c_ctx, mod_w, mod_b):
    L, D, D6 = mod_w.shape
    B = c.shape[0]
    assert B + 1 <= MOD_ROWS
    cond = jnp.zeros((MOD_ROWS, D), F32).at[:B].set(c).at[B].set(c_ctx)
    bn = _pick(D6, 1024)
    out = pl.pallas_call(
        _adaln_kernel,
        grid=(L, D6 // bn),
        in_specs=[pl.BlockSpec((MOD_ROWS, D), lambda l, j: (0, 0)),
                  pl.BlockSpec((1, D, bn), lambda l, j: (l, 0, j)),
                  pl.BlockSpec((1, 1, bn), lambda l, j: (l, 0, j))],
        out_specs=pl.BlockSpec((1, MOD_ROWS, bn), lambda l, j: (l, 0, j)),
        out_shape=jax.ShapeDtypeStruct((L, MOD_ROWS, D6), F32),
        compiler_params=_cparams("parallel", "parallel"),
        name="adaln",
    )(cond, mod_w, mod_b.reshape(L, 1, D6))
    return out.reshape(L * MOD_ROWS * 6, 1, D)


class _Rows:
    def __init__(self, B, S, Tc):
        self.B, self.S, self.Tc = B, S, Tc
        self.ML, self.MC = B * S, B * Tc
        self.M = self.ML + self.MC

    def tile(self, pref):
        return _pick(math.gcd(self.S, self.MC), pref)

    def mod_index(self, layer, which, row):
        mi = jnp.where(row >= self.ML, self.B, row // self.S)
        return (layer * MOD_ROWS + mi) * 6 + which


def _norm_mod(h_ref, g_ref, sc_ref, sh_ref):
    x = h_ref[...]
    y = x * lax.rsqrt(jnp.mean(x * x, axis=-1, keepdims=True) + EPS)
    return (y * g_ref[...]) * (1.0 + sc_ref[0]) + sh_ref[0]


def _norm_kernel(h_ref, g_ref, sc_ref, sh_ref, o_ref):
    o_ref[...] = _norm_mod(h_ref, g_ref, sc_ref, sh_ref).astype(o_ref.dtype)


def _norm_router_kernel(h_ref, g_ref, sc_ref, sh_ref, rhi_ref, rlo_ref, o_ref, route_ref, *, n_experts):
    n = _norm_mod(h_ref, g_ref, sc_ref, sh_ref)
    o_ref[...] = n
    nh = n.astype(BF16)
    nl = (n - nh.astype(F32)).astype(BF16)
    lg = _dot(nh, rhi_ref[...]) + (_dot(nl, rhi_ref[...]) + _dot(nh, rlo_ref[...]))
    lane = lax.broadcasted_iota(jnp.int32, lg.shape, 1).astype(F32)
    neg = jnp.float32(-jnp.inf)
    lg = jnp.where(lane < n_experts, lg, neg)
    m1 = jnp.max(lg, axis=-1, keepdims=True)
    i1 = jnp.min(jnp.where(lg == m1, lane, float(LANES)), axis=-1, keepdims=True)
    sel1 = lane == i1
    lg2 = jnp.where(sel1, neg, lg)
    m2 = jnp.max(lg2, axis=-1, keepdims=True)
    i2 = jnp.min(jnp.where(lg2 == m2, lane, float(LANES)), axis=-1, keepdims=True)
    e2 = jnp.exp(m2 - m1)
    w1 = 1.0 / (1.0 + e2)
    w2 = e2 / (1.0 + e2)
    route_ref[...] = jnp.where(lane == 0.0, i1, jnp.where(lane == 1.0, i2,
                               jnp.where(lane == 2.0, w1, jnp.where(lane == 3.0, w2, 0.0))))


def _norm(h, g, mod, rows, layer, sh_idx, sc_idx, m_act, router=None):
    D = h.shape[1]
    bm = rows.tile(512)
    row_spec = pl.BlockSpec((bm, D), lambda i: (i, 0))
    in_specs = [row_spec,
                pl.BlockSpec((1, D), lambda i: (0, 0)),
                pl.BlockSpec((1, 1, D), lambda i: (rows.mod_index(layer, sc_idx, i * bm), 0, 0)),
                pl.BlockSpec((1, 1, D), lambda i: (rows.mod_index(layer, sh_idx, i * bm), 0, 0))]
    args = [h, g.reshape(1, D), mod, mod]
    if router is None:
        return pl.pallas_call(
            _norm_kernel, grid=(m_act // bm,), in_specs=in_specs, out_specs=row_spec,
            out_shape=jax.ShapeDtypeStruct((m_act, D), BF16),
            compiler_params=_cparams("parallel"), name="norm_mod",
        )(*args)
    n_experts = router.shape[1]
    rpad = jnp.zeros((D, LANES), F32).at[:, :n_experts].set(router)
    rhi = rpad.astype(BF16)
    rlo = (rpad - rhi.astype(F32)).astype(BF16)
    full = pl.BlockSpec((D, LANES), lambda i: (0, 0))
    return pl.pallas_call(
        functools.partial(_norm_router_kernel, n_experts=n_experts),
        grid=(m_act // bm,), in_specs=in_specs + [full, full],
        out_specs=[row_spec, pl.BlockSpec((bm, LANES), lambda i: (i, 0))],
        out_shape=[jax.ShapeDtypeStruct((m_act, D), F32), jax.ShapeDtypeStruct((m_act, LANES), F32)],
        compiler_params=_cparams("parallel"), name="norm_mod_router",
    )(*args, rhi, rlo)


def _final_norm_kernel(h_ref, g_ref, o_ref):
    x = h_ref[...]
    y = x * lax.rsqrt(jnp.mean(x * x, axis=-1, keepdims=True) + EPS)
    o_ref[...] = y * g_ref[...]


def _final_norm(h, g, m_act):
    D = h.shape[1]
    bm = _pick(m_act, 512)
    return pl.pallas_call(
        _final_norm_kernel, grid=(m_act // bm,),
        in_specs=[pl.BlockSpec((bm, D), lambda i: (i, 0)), pl.BlockSpec((1, D), lambda i: (0, 0))],
        out_specs=pl.BlockSpec((bm, D), lambda i: (i, 0)),
        out_shape=jax.ShapeDtypeStruct((m_act, D), F32),
        compiler_params=_cparams("parallel"), name="final_norm",
    )(h, g.reshape(1, D))


def _mm_kernel(a_ref, w_ref, o_ref, *acc, nk):
    if nk == 1:
        o_ref[...] = _dot(a_ref[...], w_ref[...]).astype(o_ref.dtype)
        return
    acc_ref, = acc
    k = pl.program_id(2)

    @pl.when(k == 0)
    def _():
        acc_ref[...] = jnp.zeros_like(acc_ref)

    acc_ref[...] += _dot(a_ref[...], w_ref[...])

    @pl.when(k == nk - 1)
    def _():
        o_ref[...] = acc_ref[...].astype(o_ref.dtype)


def _stacked(w):
    return w if isinstance(w, tuple) else (w[None], 0)


def _mm(a, w, out_dtype, m_act, bm=1024, bn=1024, bk=None):
    w, wl = _stacked(w)
    _, K, N = w.shape
    bm, bn = _pick(m_act, bm), _pick(N, bn)
    bk = K if bk is None else _pick(K, bk)
    nk = K // bk
    return pl.pallas_call(
        functools.partial(_mm_kernel, nk=nk),
        grid=(N // bn, m_act // bm, nk),
        in_specs=[pl.BlockSpec((bm, bk), lambda j, i, k: (i, k)),
                  pl.BlockSpec((None, bk, bn), lambda j, i, k: (wl, k, j))],
        out_specs=pl.BlockSpec((bm, bn), lambda j, i, k: (i, j)),
        out_shape=jax.ShapeDtypeStruct((m_act, N), out_dtype),
        scratch_shapes=[pltpu.VMEM((bm, bn), F32)] if nk > 1 else [],
        compiler_params=_cparams("parallel", "parallel", "arbitrary"), name="mm",
    )(a, w)


def _mm_res_kernel(a_ref, w_ref, h_ref, gate_ref, o_ref, *acc, nk):
    if nk == 1:
        o_ref[...] = h_ref[...] + gate_ref[0] * _dot(a_ref[...], w_ref[...])
        return
    acc_ref, = acc
    k = pl.program_id(2)

    @pl.when(k == 0)
    def _():
        acc_ref[...] = jnp.zeros_like(acc_ref)

    acc_ref[...] += _dot(a_ref[...], w_ref[...])

    @pl.when(k == nk - 1)
    def _():
        o_ref[...] = h_ref[...] + gate_ref[0] * acc_ref[...]


def _mm_res(a, w, h, mod, rows, layer, gate_idx, m_rows, h_row_off=0, bm=1024, bn=1024, bk=None):
    w, wl = _stacked(w)
    _, K, N = w.shape
    bm, bn = _pick(math.gcd(rows.S, rows.MC, m_rows, h_row_off or m_rows), bm), _pick(N, bn)
    bk = K if bk is None else _pick(K, bk)
    nk = K // bk
    off = h_row_off // bm
    return pl.pallas_call(
        functools.partial(_mm_res_kernel, nk=nk),
        grid=(N // bn, m_rows // bm, nk),
        in_specs=[pl.BlockSpec((bm, bk), lambda j, i, k: (i, k)),
                  pl.BlockSpec((None, bk, bn), lambda j, i, k: (wl, k, j)),
                  pl.BlockSpec((bm, bn), lambda j, i, k: (i + off, j)),
                  pl.BlockSpec((1, 1, bn),
                               lambda j, i, k: (rows.mod_index(layer, gate_idx, (i + off) * bm), 0, j))],
        out_specs=pl.BlockSpec((bm, bn), lambda j, i, k: (i + off, j)),
        out_shape=jax.ShapeDtypeStruct(h.shape, F32),
        scratch_shapes=[pltpu.VMEM((bm, bn), F32)] if nk > 1 else [],
        input_output_aliases={2: 0},
        compiler_params=_cparams("parallel", "parallel", "arbitrary"), name="mm_residual",
    )(a, w, h, mod)


def _swiglu_kernel(a_ref, wg_ref, wu_ref, o_ref):
    a = a_ref[...]
    o_ref[...] = (_silu(_dot(a, wg_ref[...])) * _dot(a, wu_ref[...])).astype(o_ref.dtype)


def _swiglu(n, w_gu, m_act, bm=1024, bn=512):
    w_gu, wl = _stacked(w_gu)
    _, D, F2 = w_gu.shape
    F = F2 // 2
    bm, bn = _pick(m_act, bm), _pick(F, bn)
    nf = F // bn
    return pl.pallas_call(
        _swiglu_kernel,
        grid=(nf, m_act // bm),
        in_specs=[pl.BlockSpec((bm, D), lambda j, i: (i, 0)),
                  pl.BlockSpec((None, D, bn), lambda j, i: (wl, 0, j)),
                  pl.BlockSpec((None, D, bn), lambda j, i: (wl, 0, nf + j))],
        out_specs=pl.BlockSpec((bm, bn), lambda j, i: (i, j)),
        out_shape=jax.ShapeDtypeStruct((m_act, F), BF16),
        compiler_params=_cparams("parallel", "parallel"), name="swiglu",
    )(n, w_gu, w_gu)


MOE_TILE = 1024
GATHER_ROWS = 256


def _moe_plan(route, n_experts):
    m = route.shape[0]
    P, TM, E = TOP_K * m, MOE_TILE, n_experts
    e_flat = route[:, :TOP_K].astype(jnp.int32).reshape(P)
    onehot = (e_flat[:, None] == jnp.arange(E, dtype=jnp.int32)[None, :]).astype(jnp.int32)
    incl = jnp.cumsum(onehot, axis=0)
    rank = jnp.sum((incl - onehot) * onehot, axis=1)
    ptiles = (incl[-1] + TM - 1) // TM
    tile_end = jnp.cumsum(ptiles)
    dest = (tile_end - ptiles)[e_flat] * TM + rank
    n_tiles = P // TM + E
    tile_expert = jnp.minimum(jnp.searchsorted(tile_end, jnp.arange(n_tiles), side="right"), E - 1).astype(jnp.int32)
    tok = jnp.arange(P, dtype=jnp.int32) // TOP_K
    src = jnp.zeros((n_tiles * TM,), jnp.int32).at[dest].set(tok)
    return src, dest.astype(jnp.int32), tile_expert, tile_end[-1:].astype(jnp.int32)


def _row_gather_pipeline(issue, wait, n_steps):
    i = pl.program_id(0)
    slot = i % 2

    @pl.when(i == 0)
    def _():
        issue(False, 0)

    @pl.when(i + 1 < n_steps)
    def _():
        issue(True, 1 - slot)

    wait(slot)
    return slot


def _gather_kernel(idx_ref, nxt_ref, src_ref, o_ref, buf_ref, sem, *, G, n_steps):
    def copy(ref, r, s):
        return pltpu.make_async_copy(src_ref.at[pl.ds(ref[0, 0, r], 1)], buf_ref.at[s, pl.ds(r, 1)], sem.at[s])

    def issue(ahead, s):
        ref = nxt_ref if ahead else idx_ref

        def body(r, carry):
            copy(ref, r, s).start()
            return carry

        lax.fori_loop(0, G, body, 0, unroll=8)

    def wait(s):
        def body(r, carry):
            copy(idx_ref, r, s).wait()
            return carry

        lax.fori_loop(0, G, body, 0, unroll=8)

    slot = _row_gather_pipeline(issue, wait, n_steps)
    o_ref[...] = buf_ref[slot].astype(o_ref.dtype)


def _gather_rows(x, idx):
    D = x.shape[1]
    G = GATHER_ROWS
    R = idx.shape[0]
    n = R // G
    smem = lambda f: pl.BlockSpec((1, 1, G), f, memory_space=pltpu.SMEM)
    return pl.pallas_call(
        functools.partial(_gather_kernel, G=G, n_steps=n),
        grid=(n,),
        in_specs=[smem(lambda i: (i, 0, 0)), smem(lambda i: (jnp.minimum(i + 1, n - 1), 0, 0)),
                  pl.BlockSpec(memory_space=pl.ANY)],
        out_specs=pl.BlockSpec((G, D), lambda i: (i, 0)),
        out_shape=jax.ShapeDtypeStruct((R, D), BF16),
        scratch_shapes=[pltpu.VMEM((2, G, D), F32), pltpu.SemaphoreType.DMA((2,))],
        compiler_params=_cparams("arbitrary"), name="moe_dispatch_gather",
    )(idx.reshape(n, 1, G), idx.reshape(n, 1, G), x)


def _moe_swiglu_kernel(te_ref, nu_ref, x_ref, wg_ref, wu_ref, o_ref, wgb_ref, wub_ref):
    i = pl.program_id(1)
    active = i < nu_ref[0]

    @pl.when(jnp.logical_or(i == 0, te_ref[i] != te_ref[jnp.maximum(i - 1, 0)]))
    def _():
        wgb_ref[...] = wg_ref[...].astype(BF16)
        wub_ref[...] = wu_ref[...].astype(BF16)

    @pl.when(active)
    def _():
        a = x_ref[...]
        o_ref[...] = (_silu(_dot(a, wgb_ref[...])) * _dot(a, wub_ref[...])).astype(o_ref.dtype)

    @pl.when(jnp.logical_not(active))
    def _():
        o_ref[...] = jnp.zeros_like(o_ref)


def _moe_down_kernel(te_ref, nu_ref, a_ref, w_ref, o_ref):
    active = pl.program_id(1) < nu_ref[0]

    @pl.when(active)
    def _():
        o_ref[...] = _dot(a_ref[...], w_ref[...])

    @pl.when(jnp.logical_not(active))
    def _():
        o_ref[...] = jnp.zeros_like(o_ref)


def _moe_experts(xs, tile_expert, n_used, w_gu, w_down, m, bn=512, bn_down=512):
    _, E, D, F2 = w_gu.shape
    F = F2 // 2
    R = xs.shape[0]
    TM = MOE_TILE
    bn = _pick(F, bn)
    nf = F // bn
    h1 = pl.pallas_call(
        _moe_swiglu_kernel,
        grid_spec=pltpu.PrefetchScalarGridSpec(
            num_scalar_prefetch=2, grid=(nf, R // TM),
            in_specs=[pl.BlockSpec((TM, D), lambda j, i, te, nu: (i, 0)),
                      pl.BlockSpec((None, None, D, bn), lambda j, i, te, nu: (m, te[i], 0, j)),
                      pl.BlockSpec((None, None, D, bn), lambda j, i, te, nu: (m, te[i], 0, nf + j))],
            out_specs=pl.BlockSpec((TM, bn), lambda j, i, te, nu: (i, j)),
            scratch_shapes=[pltpu.VMEM((D, bn), BF16), pltpu.VMEM((D, bn), BF16)]),
        out_shape=jax.ShapeDtypeStruct((R, F), BF16),
        compiler_params=_cparams("arbitrary", "arbitrary"), name="moe_swiglu",
    )(tile_expert, n_used, xs, w_gu, w_gu)
    bnd = _pick(D, bn_down)
    return pl.pallas_call(
        _moe_down_kernel,
        grid_spec=pltpu.PrefetchScalarGridSpec(
            num_scalar_prefetch=2, grid=(D // bnd, R // TM),
            in_specs=[pl.BlockSpec((TM, F), lambda j, i, te, nu: (i, 0)),
                      pl.BlockSpec((None, None, F, bnd), lambda j, i, te, nu: (m, te[i], 0, j))],
            out_specs=pl.BlockSpec((TM, bnd), lambda j, i, te, nu: (i, j))),
        out_shape=jax.ShapeDtypeStruct((R, D), F32),
        compiler_params=_cparams("parallel", "arbitrary"), name="moe_down",
    )(tile_expert, n_used, h1, w_down)


def _combine_kernel(idx_ref, nxt_ref, y_ref, route_ref, h_ref, gate_ref, *rest, G, n_steps, norm):
    if norm == "mod":
        g_ref, sc_ref, sh_ref, o_ref, n_ref, buf_ref, sem = rest
    else:
        g_ref, n_ref, buf_ref, sem = rest

    def copy(ref, r, k, s):
        return pltpu.make_async_copy(y_ref.at[pl.ds(ref[0, 0, TOP_K * r + k], 1)],
                                     buf_ref.at[s, k, pl.ds(r, 1)], sem.at[s])

    def issue(ahead, s):
        ref = nxt_ref if ahead else idx_ref

        def body(r, carry):
            for k in range(TOP_K):
                copy(ref, r, k, s).start()
            return carry

        lax.fori_loop(0, G, body, 0, unroll=4)

    def wait(s):
        def body(r, carry):
            for k in range(TOP_K):
                copy(idx_ref, r, k, s).wait()
            return carry

        lax.fori_loop(0, G, body, 0, unroll=4)

    slot = _row_gather_pipeline(issue, wait, n_steps)
    y = route_ref[:, TOP_K:TOP_K + 1] * buf_ref[slot, 0]
    for k in range(1, TOP_K):
        y = y + route_ref[:, TOP_K + k:TOP_K + k + 1] * buf_ref[slot, k]
    hn = h_ref[...] + gate_ref[0] * y
    z = (hn * lax.rsqrt(jnp.mean(hn * hn, axis=-1, keepdims=True) + EPS)) * g_ref[...]
    if norm == "mod":
        o_ref[...] = hn
        z = z * (1.0 + sc_ref[0]) + sh_ref[0]
    n_ref[...] = z.astype(n_ref.dtype)


def _moe_combine(y, dest, route, h, mod, rows, layer, gate_idx, m_act, next_norm):
    D = h.shape[1]
    G = rows.tile(GATHER_ROWS)
    n = m_act // G
    smem = lambda f: pl.BlockSpec((1, 1, TOP_K * G), f, memory_space=pltpu.SMEM)
    idx = dest.reshape(n, 1, TOP_K * G)
    row_spec = pl.BlockSpec((G, D), lambda i: (i, 0))
    vec_spec = pl.BlockSpec((1, D), lambda i: (0, 0))
    in_specs = [smem(lambda i: (i, 0, 0)), smem(lambda i: (jnp.minimum(i + 1, n - 1), 0, 0)),
                pl.BlockSpec(memory_space=pl.ANY),
                pl.BlockSpec((G, LANES), lambda i: (i, 0)),
                row_spec,
                pl.BlockSpec((1, 1, D), lambda i: (rows.mod_index(layer, gate_idx, i * G), 0, 0))]
    args = [idx, idx, y, route, h, mod]
    h_shape = jax.ShapeDtypeStruct(h.shape, F32)
    kind, g = next_norm[0], next_norm[1].reshape(1, D)
    if kind == "mod":
        nl = next_norm[2]
        in_specs += [vec_spec,
                     pl.BlockSpec((1, 1, D), lambda i: (rows.mod_index(nl, 1, i * G), 0, 0)),
                     pl.BlockSpec((1, 1, D), lambda i: (rows.mod_index(nl, 0, i * G), 0, 0))]
        args += [g, mod, mod]
        out_specs, out_shape, aliases = [row_spec, row_spec], [h_shape, jax.ShapeDtypeStruct((m_act, D), BF16)], {4: 0}
    else:
        in_specs += [vec_spec]
        args += [g]
        out_specs, out_shape, aliases = row_spec, jax.ShapeDtypeStruct((m_act, D), F32), {}
    return pl.pallas_call(
        functools.partial(_combine_kernel, G=G, n_steps=n, norm=kind),
        grid=(n,),
        in_specs=in_specs, out_specs=out_specs, out_shape=out_shape,
        scratch_shapes=[pltpu.VMEM((2, TOP_K, G, D), F32), pltpu.SemaphoreType.DMA((2,))],
        input_output_aliases=aliases,
        compiler_params=_cparams("arbitrary"), name="moe_combine",
    )(*args)


def _dft_cos_sin(n):
    k = np.arange(n, dtype=np.int64)
    ang = 2.0 * np.pi * ((k[:, None] * k[None, :]) % n).astype(np.float64) / n
    return np.cos(ang), np.sin(ang)


def _fnet_chan_kernel(u_ref, w_ref, o_ref, *, groups, gd):
    for g in range(groups):
        r = _dot(u_ref[:, g * gd:(g + 1) * gd], w_ref[...])
        o_ref[0, :, g * gd:(g + 1) * gd] = r[:, :gd].astype(o_ref.dtype)
        o_ref[1, :, g * gd:(g + 1) * gd] = r[:, gd:].astype(o_ref.dtype)


def _fnet_chan(u, m_act):
    D = u.shape[1]
    gd = FNET_GROUP_DIM
    c, s = _dft_cos_sin(gd)
    w = jnp.asarray(np.concatenate([c, -s], axis=1) / math.sqrt(gd), BF16)
    bm = _pick(m_act, 512)
    return pl.pallas_call(
        functools.partial(_fnet_chan_kernel, groups=D // gd, gd=gd),
        grid=(m_act // bm,),
        in_specs=[pl.BlockSpec((bm, D), lambda i: (i, 0)), pl.BlockSpec((gd, 2 * gd), lambda i: (0, 0))],
        out_specs=pl.BlockSpec((2, bm, D), lambda i: (0, i, 0)),
        out_shape=jax.ShapeDtypeStruct((2, m_act, D), BF16),
        compiler_params=_cparams("parallel"), name="fnet_chan_dft",
    )(u, w)


def _fft1_kernel(x_ref, w_ref, tw_ref, o_ref, *, n1, nb, D):
    x2 = x_ref[...].reshape(2 * n1, nb * D)
    y = _dot(w_ref[...], x2)
    yr, yi = y[:n1], y[n1:]
    for t in range(nb):
        twr = tw_ref[0, 0, :, t:t + 1]
        twi = tw_ref[1, 0, :, t:t + 1]
        a, b = yr[:, t * D:(t + 1) * D], yi[:, t * D:(t + 1) * D]
        o_ref[0, 0, t] = (a * twr - b * twi).astype(o_ref.dtype)
        o_ref[0, 1, t] = (a * twi + b * twr).astype(o_ref.dtype)


def _fft3_kernel(z_ref, w_ref, o_ref, *, n2, nb, D):
    for t in range(nb):
        z = z_ref[0, :, :, t * D:(t + 1) * D].reshape(2 * n2, D)
        o_ref[0, :, t * D:(t + 1) * D] = _dot(w_ref[...], z).astype(o_ref.dtype)


def _fnet_seq_latent(p, rows):
    D = p.shape[2]
    B, S = rows.B, rows.S
    n1 = 64
    n2 = S // n1
    assert n1 * n2 == S and n2 % 8 == 0 and rows.MC % n2 == 0
    nb = 8
    c1, s1 = _dft_cos_sin(n1)
    w1 = jnp.asarray(np.block([[c1, s1], [-s1, c1]]) / math.sqrt(n1), BF16)
    f = np.arange(n1)[:, None] * np.arange(n2)[None, :]
    ang = 2.0 * np.pi * f / S
    tw = np.stack([np.cos(ang), -np.sin(ang)]).reshape(2, n1, n2 // nb, nb).transpose(0, 2, 1, 3)
    tw = jnp.asarray(tw, F32)
    pv = p.reshape(2, p.shape[1] // n2, n2 * D)
    y = pl.pallas_call(
        functools.partial(_fft1_kernel, n1=n1, nb=nb, D=D),
        grid=(B, n2 // nb),
        in_specs=[pl.BlockSpec((2, n1, nb * D), lambda b, j: (0, b, j)),
                  pl.BlockSpec((2 * n1, 2 * n1), lambda b, j: (0, 0)),
                  pl.BlockSpec((2, 1, n1, nb), lambda b, j: (0, j, 0, 0))],
        out_specs=pl.BlockSpec((1, 2, nb, n1, D), lambda b, j: (b, 0, j, 0, 0)),
        out_shape=jax.ShapeDtypeStruct((B, 2, n2, n1, D), BF16),
        compiler_params=_cparams("parallel", "parallel"), name="fnet_fft_stage1",
    )(pv, w1, tw)
    c2, s2 = _dft_cos_sin(n2)
    w3 = jnp.asarray(np.concatenate([c2, s2], axis=1) / math.sqrt(n2), BF16)
    yv = y.reshape(B, 2, n2, n1 * D)
    out = pl.pallas_call(
        functools.partial(_fft3_kernel, n2=n2, nb=nb, D=D),
        grid=(B, n1 // nb),
        in_specs=[pl.BlockSpec((1, 2, n2, nb * D), lambda b, j: (b, 0, 0, j)),
                  pl.BlockSpec((n2, 2 * n2), lambda b, j: (0, 0))],
        out_specs=pl.BlockSpec((1, n2, nb * D), lambda b, j: (b, 0, j)),
        out_shape=jax.ShapeDtypeStruct((B, n2, n1 * D), BF16),
        compiler_params=_cparams("parallel", "parallel"), name="fnet_fft_stage2",
    )(yv, w3)
    return out.reshape(B * S, D)


def _dft_ctx_kernel(z_ref, w_ref, o_ref, *, T, D):
    z = z_ref[...].reshape(2 * T, D)
    o_ref[...] = _dot(w_ref[...], z).astype(o_ref.dtype)


def _fnet_seq_ctx(p, rows):
    D = p.shape[2]
    T = rows.Tc
    c, s = _dft_cos_sin(T)
    w = jnp.asarray(np.concatenate([c, s], axis=1) / math.sqrt(T), BF16)
    off = rows.ML // T
    return pl.pallas_call(
        functools.partial(_dft_ctx_kernel, T=T, D=D),
        grid=(rows.B,),
        in_specs=[pl.BlockSpec((2, T, D), lambda b: (0, off + b, 0)),
                  pl.BlockSpec((T, 2 * T), lambda b: (0, 0))],
        out_specs=pl.BlockSpec((T, D), lambda b: (b, 0)),
        out_shape=jax.ShapeDtypeStruct((rows.MC, D), BF16),
        compiler_params=_cparams("parallel"), name="fnet_dft_ctx",
    )(p, w)


def _fourier_mixer(n, h, w_in, w_out, mod, rows, layer, need_ctx):
    m_act = rows.M if need_ctx else rows.ML
    u = _mm(n, w_in, BF16, m_act)
    p = _fnet_chan(u, m_act)
    f_lat = _fnet_seq_latent(p, rows)
    h = _mm_res(f_lat, w_out, h, mod, rows, layer, 2, rows.ML)
    if need_ctx:
        f_ctx = _fnet_seq_ctx(p, rows)
        h = _mm_res(f_ctx, w_out, h, mod, rows, layer, 2, rows.MC, h_row_off=rows.ML)
    return h


def _mla_proj_kernel(z_ref, qn_ref, kvn_ref, wq_ref, wqs_ref, wkv_ref, cos_ref, sin_ref,
                     q_ref, k_ref, v_ref, *, H, qr, kvr, scale):
    def rms(x, g):
        return ((x * lax.rsqrt(jnp.mean(x * x, axis=-1, keepdims=True) + EPS)) * g).astype(BF16)

    qn = rms(z_ref[:, :qr], qn_ref[...])
    kvn = rms(z_ref[:, qr:qr + kvr], kvn_ref[...])
    kr = z_ref[:, qr + kvr:qr + kvr + LANES]
    kr_sw = z_ref[:, qr + kvr + LANES:qr + kvr + 2 * LANES]
    cos, sin = cos_ref[...], sin_ref[...]
    k_rope = (kr * cos + kr_sw * sin).astype(BF16)
    q = _dot(qn, wq_ref[...]) * scale
    q_sw = _dot(qn, wqs_ref[...]) * scale
    kv = _dot(kvn, wkv_ref[...])
    P = MLA_HEAD_PAD
    for hh in range(H):
        q_ref[0, hh, :, :LANES] = q[:, hh * P:hh * P + LANES].astype(BF16)
        q_ref[0, hh, :, LANES:] = (q[:, hh * P + LANES:(hh + 1) * P] * cos
                                   + q_sw[:, hh * LANES:(hh + 1) * LANES] * sin).astype(BF16)
        k_ref[0, hh, :, :LANES] = kv[:, hh * LANES:(hh + 1) * LANES].astype(BF16)
        k_ref[0, hh, :, LANES:] = k_rope
        v_ref[0, hh] = kv[:, (H + hh) * LANES:(H + hh + 1) * LANES].T.astype(BF16)


def _mla_attn_kernel(q_ref, k_ref, vt_ref, o_ref):
    for hh in range(q_ref.shape[1]):
        s_t = _dot_nt(k_ref[0, hh], q_ref[0, hh])
        m = jnp.max(s_t, axis=0, keepdims=True)
        p = jnp.exp2(s_t - m)
        l = jnp.sum(p, axis=0, keepdims=True)
        o_t = _dot(vt_ref[0, hh], p.astype(BF16)) / l
        o_ref[:, hh * MLA_V:(hh + 1) * MLA_V] = o_t.T.astype(o_ref.dtype)


def _axial_rope_table(S, Tc):
    half = MLA_ROPE // 2
    rows_ = S // GRID_W
    row = jnp.repeat(jnp.arange(rows_, dtype=F32), GRID_W)
    col = jnp.tile(jnp.arange(GRID_W, dtype=F32), rows_)
    n_freq = MLA_ROPE // 4
    inv = ROPE_BASE ** (-jnp.arange(n_freq, dtype=F32) / n_freq)
    ang = jnp.concatenate([row[:, None] * inv, col[:, None] * inv], axis=-1)
    cos, sin = jnp.cos(ang), jnp.sin(ang)
    zpad = jnp.zeros((S, LANES - 2 * half), F32)
    cos_l = jnp.concatenate([cos, cos, zpad], axis=-1)
    sin_l = jnp.concatenate([sin, sin, zpad], axis=-1)
    cos_c = jnp.concatenate([jnp.ones((Tc, 2 * half), F32), jnp.zeros((Tc, LANES - 2 * half), F32)], axis=-1)
    return jnp.concatenate([cos_l, cos_c], axis=0), jnp.concatenate([sin_l, jnp.zeros((Tc, LANES), F32)], axis=0)


def _swap_halves_neg(w):
    half = w.shape[-1] // 2
    return jnp.concatenate([-w[..., half:], w[..., :half]], axis=-1)


def _mla_mixer(n, h, w_in, q_norm, w_uq, kv_norm, w_ukv, w_o, mod, rows, layer):
    B, S, Tc = rows.B, rows.S, rows.Tc
    D = n.shape[1]
    qr, kvr = q_norm.shape[0], kv_norm.shape[0]
    H = w_uq.shape[1] // (MLA_NOPE + MLA_ROPE)
    C = CHUNK
    assert Tc == C and S % C == 0 and MLA_NOPE == LANES and MLA_V == LANES
    pad = LANES - MLA_ROPE
    w_kr = w_in[:, qr + kvr:]
    zc = jnp.zeros((D, pad), F32)
    w_in_p = jnp.concatenate([w_in[:, :qr + kvr], w_kr, zc, _swap_halves_neg(w_kr), zc], axis=1).astype(BF16)
    wq = w_uq.reshape(qr, H, MLA_NOPE + MLA_ROPE)
    zq = jnp.zeros((qr, H, pad), F32)
    wq_p = jnp.concatenate([wq, zq], axis=-1).reshape(qr, H * MLA_HEAD_PAD).astype(BF16)
    wq_sw = jnp.concatenate([_swap_halves_neg(wq[..., MLA_NOPE:]), zq], axis=-1).reshape(qr, H * LANES).astype(BF16)
    wkv = w_ukv.reshape(kvr, H, MLA_NOPE + MLA_V)
    wkv_p = jnp.concatenate([wkv[..., :MLA_NOPE].reshape(kvr, H * LANES),
                             wkv[..., MLA_NOPE:].reshape(kvr, H * LANES)], axis=1).astype(BF16)
    cos_t, sin_t = _axial_rope_table(S, Tc)

    z = _mm(n, w_in_p, F32, rows.M)
    nS = S // C
    n_lat_tiles = rows.ML // C
    T = S + Tc

    def bt(i):
        lat = i < n_lat_tiles
        return jnp.where(lat, i // nS, i - n_lat_tiles), jnp.where(lat, i % nS, nS)

    def kv_map(i):
        b, t = bt(i)
        return (b, 0, t, 0)

    ZW = w_in_p.shape[1]
    const = lambda shape: pl.BlockSpec(shape, lambda i: (0,) * len(shape))
    q, k, v_t = pl.pallas_call(
        functools.partial(_mla_proj_kernel, H=H, qr=qr, kvr=kvr,
                          scale=float((MLA_NOPE + MLA_ROPE) ** -0.5) * math.log2(math.e)),
        grid=(rows.M // C,),
        in_specs=[pl.BlockSpec((C, ZW), lambda i: (i, 0)),
                  const((1, qr)), const((1, kvr)),
                  const(wq_p.shape), const(wq_sw.shape), const(wkv_p.shape),
                  pl.BlockSpec((C, LANES), lambda i: (bt(i)[1], 0)),
                  pl.BlockSpec((C, LANES), lambda i: (bt(i)[1], 0))],
        out_specs=[pl.BlockSpec((1, H, C, MLA_HEAD_PAD), kv_map),
                   pl.BlockSpec((1, H, C, MLA_HEAD_PAD), kv_map),
                   pl.BlockSpec((1, H, MLA_V, C), lambda i: (bt(i)[0], 0, 0, bt(i)[1]))],
        out_shape=[jax.ShapeDtypeStruct((B, H, T, MLA_HEAD_PAD), BF16),
                   jax.ShapeDtypeStruct((B, H, T, MLA_HEAD_PAD), BF16),
                   jax.ShapeDtypeStruct((B, H, MLA_V, T), BF16)],
        compiler_params=_cparams("parallel"), name="mla_project",
    )(z, q_norm.reshape(1, qr), kv_norm.reshape(1, kvr), wq_p, wq_sw, wkv_p, cos_t, sin_t)

    bq = _pick(S, 512)
    nq = S // bq
    HB = 2 if H % 2 == 0 else 1
    o_lat = pl.pallas_call(
        _mla_attn_kernel,
        grid=(B, H // HB, nq),
        in_specs=[pl.BlockSpec((1, HB, bq, MLA_HEAD_PAD), lambda b, hh, i: (b, hh, i, 0)),
                  pl.BlockSpec((1, HB, T, MLA_HEAD_PAD), lambda b, hh, i: (b, hh, 0, 0)),
                  pl.BlockSpec((1, HB, MLA_V, T), lambda b, hh, i: (b, hh, 0, 0))],
        out_specs=pl.BlockSpec((bq, HB * MLA_V), lambda b, hh, i: (b * nq + i, hh)),
        out_shape=jax.ShapeDtypeStruct((rows.ML, H * MLA_V), BF16),
        compiler_params=_cparams("parallel", "parallel", "arbitrary"), name="mla_attention",
    )(q, k, v_t)
    o_ctx = pl.pallas_call(
        _mla_attn_kernel,
        grid=(B, H),
        in_specs=[pl.BlockSpec((1, 1, Tc, MLA_HEAD_PAD), lambda b, hh: (b, hh, nS, 0)),
                  pl.BlockSpec((1, 1, Tc, MLA_HEAD_PAD), lambda b, hh: (b, hh, nS, 0)),
                  pl.BlockSpec((1, 1, MLA_V, Tc), lambda b, hh: (b, hh, 0, nS))],
        out_specs=pl.BlockSpec((Tc, MLA_V), lambda b, hh: (b, hh)),
        out_shape=jax.ShapeDtypeStruct((rows.MC, H * MLA_V), BF16),
        compiler_params=_cparams("parallel", "parallel"), name="mla_attention_ctx",
    )(q, k, v_t)
    h = _mm_res(o_lat, w_o, h, mod, rows, layer, 2, rows.ML)
    return _mm_res(o_ctx, w_o, h, mod, rows, layer, 2, rows.MC, h_row_off=rows.ML)


def _ret_kernel(dr_ref, q_ref, k_ref, v_ref, cos_ref, sin_ref, *rest, H, C, dk, dv, backward, combine):
    if combine:
        yf_ref, gf_ref, gb_ref, o_ref, R_ref, D_ref, qd_ref, kd_ref, cd_ref = rest
    else:
        o_ref, R_ref, D_ref, qd_ref, kd_ref, cd_ref = rest
    b, s = pl.program_id(0), pl.program_id(1)
    half = dk // 2

    @pl.when((b == 0) & (s == 0))
    def _():
        ri = lax.broadcasted_iota(jnp.int32, (C, C), 0).astype(F32)
        ci = lax.broadcasted_iota(jnp.int32, (C, C), 1).astype(F32)
        diff = (ci - ri) if backward else (ri - ci)
        r = lax.broadcasted_iota(jnp.int32, (C, LANES), 0).astype(F32)
        qpow = (C - r) if backward else (r + 1.0)
        kpow = r if backward else (C - 1.0 - r)
        for hh in range(H):
            lg = jnp.log1p(-jnp.exp(dr_ref[hh]))
            D_ref[hh] = jnp.where(diff >= 0, jnp.exp(jnp.maximum(diff, 0.0) * lg[0:1, :]), 0.0)
            qd_ref[hh] = jnp.exp(qpow * lg[0:1, :LANES])
            kd_ref[hh] = jnp.exp(kpow * lg[0:1, :LANES])
            cd_ref[hh] = jnp.exp(float(C) * lg[:, :LANES])

    @pl.when(s == 0)
    def _():
        R_ref[...] = jnp.zeros_like(R_ref)

    cos, sin = cos_ref[...], sin_ref[...]

    def rope(x):
        x1, x2 = x[:, :half], x[:, half:]
        return jnp.concatenate([x1 * cos - x2 * sin, x1 * sin + x2 * cos], axis=-1)

    for hh in range(H):
        q = rope(q_ref[:, hh * dk:(hh + 1) * dk].astype(F32))
        k = rope(k_ref[:, hh * dk:(hh + 1) * dk].astype(F32)) * (dk ** -0.5)
        v = v_ref[:, hh * dv:(hh + 1) * dv]
        qd = jnp.concatenate([qd_ref[hh]] * (dk // LANES), axis=-1)
        kd = jnp.concatenate([kd_ref[hh]] * (dk // LANES), axis=-1)
        cd = jnp.concatenate([cd_ref[hh][0:1]] * (dv // LANES), axis=-1)
        a = _dot_nt(q.astype(BF16), k.astype(BF16)) * D_ref[hh]
        R = R_ref[hh]
        o = _dot(a.astype(BF16), v) + _dot((q * qd).astype(BF16), R.astype(BF16))
        R_ref[hh] = R * cd + _dot_tn((k * kd).astype(BF16), v)
        mu = jnp.mean(o, axis=-1, keepdims=True)
        d = o - mu
        y = d * lax.rsqrt(jnp.mean(d * d, axis=-1, keepdims=True) + GN_EPS)
        sl = slice(hh * dv, (hh + 1) * dv)
        if combine:
            y = (_silu(gf_ref[:, sl].astype(F32)) * yf_ref[:, sl].astype(F32)
                 + _silu(gb_ref[:, sl].astype(F32)) * y)
        o_ref[:, sl] = y.astype(o_ref.dtype)


def _retention_rope_table(S, Tc, dim):
    inv = ROPE_BASE ** (-jnp.linspace(0.0, 1.0, dim // 2, dtype=F32))
    ang = jnp.arange(S, dtype=F32)[:, None] * inv
    cos = jnp.concatenate([jnp.cos(ang), jnp.ones((Tc, dim // 2), F32)], axis=0)
    sin = jnp.concatenate([jnp.sin(ang), jnp.zeros((Tc, dim // 2), F32)], axis=0)
    return cos, sin


def _retention_mixer(n, h, w_in, decay_rate, w_o, mod, rows, layer):
    B, S, Tc = rows.B, rows.S, rows.Tc
    D = n.shape[1]
    H = RET_HEADS
    dk = D // H
    dv = 2 * dk
    QK, VT = H * dk, H * dv
    C = CHUNK
    assert Tc == C and S % C == 0 and dk // 2 == LANES and w_in[0].shape[2] == 2 * QK + 3 * VT and VT == 2 * QK
    z = _mm(n, w_in, BF16, rows.M)
    cos_t, sin_t = _retention_rope_table(S, Tc, dk)
    nS = S // C
    n_lat_tiles = rows.ML // C
    dr = jnp.broadcast_to(decay_rate.astype(F32)[:, :, None, None], (2, H, 8, C))

    def run(backward, yf):
        def blk(b, s):
            j = (nS - s) if backward else (s - 1)
            return jnp.where(s == 0, n_lat_tiles + b, b * nS + j)

        def tab(b, s):
            j = (nS - s) if backward else (s - 1)
            return (jnp.where(s == 0, nS, j), 0)

        in_specs = [pl.BlockSpec((H, 8, C), lambda b, s: (0, 0, 0)),
                    pl.BlockSpec((C, QK), lambda b, s: (blk(b, s), 0)),
                    pl.BlockSpec((C, QK), lambda b, s: (blk(b, s), 1)),
                    pl.BlockSpec((C, VT), lambda b, s: (blk(b, s), 1)),
                    pl.BlockSpec((C, LANES), tab),
                    pl.BlockSpec((C, LANES), tab)]
        args = [dr[1 if backward else 0], z, z, z, cos_t, sin_t]
        if yf is not None:
            in_specs += [pl.BlockSpec((C, VT), lambda b, s: (blk(b, s), 0)),
                         pl.BlockSpec((C, VT), lambda b, s: (blk(b, s), 2)),
                         pl.BlockSpec((C, VT), lambda b, s: (blk(b, s), 3))]
            args += [yf, z, z]
        return pl.pallas_call(
            functools.partial(_ret_kernel, H=H, C=C, dk=dk, dv=dv, backward=backward, combine=yf is not None),
            grid=(B, nS + 1),
            in_specs=in_specs,
            out_specs=pl.BlockSpec((C, VT), lambda b, s: (blk(b, s), 0)),
            out_shape=jax.ShapeDtypeStruct((rows.M, VT), BF16),
            scratch_shapes=[pltpu.VMEM((H, dk, dv), F32), pltpu.VMEM((H, C, C), F32),
                            pltpu.VMEM((H, C, LANES), F32), pltpu.VMEM((H, C, LANES), F32),
                            pltpu.VMEM((H, 8, LANES), F32)],
            compiler_params=_cparams("arbitrary", "arbitrary"),
            name="retention_bwd_combine" if backward else "retention_fwd",
        )(*args)

    y_f = run(False, None)
    y = run(True, y_f)
    return _mm_res(y, w_o, h, mod, rows, layer, 2, rows.M, bm=512)


def _ffn(n, h, w_gu, w_down, mod, rows, layer, m_act):
    h1 = _swiglu(n, w_gu, m_act)
    return _mm_res(h1, w_down, h, mod, rows, layer, 5, m_act, bk=w_down[0].shape[1] // 2)


def _moe_ffn(n, route, h, w_gu, w_down, m, mod, rows, layer, m_act, next_norm):
    src, dest, tile_expert, n_used = _moe_plan(route, w_gu.shape[1])
    xs = _gather_rows(n, src)
    y = _moe_experts(xs, tile_expert, n_used, w_gu, w_down, m)
    return _moe_combine(y, dest, route, h, mod, rows, layer, 5, m_act, next_norm)


def kernel(x, c, ctx, c_ctx, mod_w, mod_b, norm_g, final_norm_g, fnet_w_in, fnet_w_out, mla_w_in, mla_q_norm, mla_w_uq, mla_kv_norm, mla_w_ukv, mla_w_o, ret_w_in, ret_decay_rate, ret_w_o, ffn_w_gu, ffn_w_down, moe_router, moe_w_gu, moe_w_down):
    B, S, D = x.shape
    Tc = ctx.shape[1]
    depth = mod_w.shape[0]
    rows = _Rows(B, S, Tc)
    h = jnp.concatenate([x.reshape(rows.ML, D), ctx.reshape(rows.MC, D)], axis=0)
    mod = _adaln(c, c_ctx, mod_w, mod_b)
    fnet_w_in, fnet_w_out, ret_w_in, ret_w_o, mla_w_o, ffn_w_gu, ffn_w_down, moe_w_down = (
        w.astype(BF16) for w in (fnet_w_in, fnet_w_out, ret_w_in, ret_w_o, mla_w_o, ffn_w_gu, ffn_w_down,
                                 moe_w_down))
    n_pre = None
    for i in range(depth):
        need_ctx = i < depth - 1
        kind, j = i % 3, i // 3
        m_act = rows.M if (need_ctx or kind != 0) else rows.ML
        n = n_pre if n_pre is not None else _norm(h, norm_g[i, 0], mod, rows, i, 0, 1, m_act)
        n_pre = None
        if kind == 0:
            h = _fourier_mixer(n, h, (fnet_w_in, j), (fnet_w_out, j), mod, rows, i, need_ctx)
        elif kind == 1:
            h = _mla_mixer(n, h, mla_w_in[j], mla_q_norm[j], mla_w_uq[j], mla_kv_norm[j], mla_w_ukv[j],
                           (mla_w_o, j), mod, rows, i)
        else:
            h = _retention_mixer(n, h, (ret_w_in, j), ret_decay_rate[j], (ret_w_o, j), mod, rows, i)
        m_act = rows.M if need_ctx else rows.ML
        m = i // 2
        if i % 2 == 0:
            n = _norm(h, norm_g[i, 1], mod, rows, i, 3, 4, m_act)
            h = _ffn(n, h, (ffn_w_gu, m), (ffn_w_down, m), mod, rows, i, m_act)
        else:
            n, route = _norm(h, norm_g[i, 1], mod, rows, i, 3, 4, m_act, router=moe_router[m])
            if i == depth - 1:
                return _moe_ffn(n, route, h, moe_w_gu, moe_w_down, m, mod, rows, i, m_act,
                                ("final", final_norm_g)).reshape(B, S, D)
            h, n_pre = _moe_ffn(n, route, h, moe_w_gu, moe_w_down, m, mod, rows, i, m_act,
                                ("mod", norm_g[i + 1, 0], i + 1))
    return _final_norm(h, final_norm_g, rows.ML).reshape(B, S, D)
```

```python
import functools
import math

import numpy as np
import jax
import jax.numpy as jnp
from jax import lax
from jax.experimental import pallas as pl
from jax.experimental.pallas import tpu as pltpu

F32 = jnp.float32
BF16 = jnp.bfloat16

VMEM_LIMIT_BYTES = 56 * 1024 * 1024
LANES = 128

EPS = 1e-6
GN_EPS = 1e-5
ROPE_BASE = 10000.0
GRID_W = 64
MOD_ROWS = 16
FNET_GROUP_DIM = 256
MLA_NOPE = 128
MLA_ROPE = 64
MLA_V = 128
MLA_HEAD_PAD = 256
RET_HEADS = 8
CHUNK = 256
TOP_K = 2


def _cparams(*sem):
    return pltpu.CompilerParams(dimension_semantics=sem, vmem_limit_bytes=VMEM_LIMIT_BYTES)


def _pick(dim, pref):
    if dim <= pref:
        return dim
    t = pref
    while t >= LANES:
        if dim % t == 0:
            return t
        t -= LANES
    return dim


def _dot(a, b):
    return jnp.dot(a, b, preferred_element_type=F32)


def _dot_nt(a, b):
    return lax.dot_general(a, b, (((1,), (1,)), ((), ())), preferred_element_type=F32)


def _dot_tn(a, b):
    return lax.dot_general(a, b, (((0,), (0,)), ((), ())), preferred_element_type=F32)


def _silu(x):
    hx = 0.5 * x
    return hx + hx * jnp.tanh(hx)


def _adaln_kernel(cond_ref, w_ref, b_ref, o_ref):
    a = _silu(cond_ref[...]).astype(BF16)
    o_ref[0] = _dot(a, w_ref[0].astype(BF16)) + b_ref[0]


def _adaln(c, c_ctx, mod_w, mod_b):
    L, D, D6 = mod_w.shape
    B = c.shape[0]
    assert B + 1 <= MOD_ROWS
    cond = jnp.zeros((MOD_ROWS, D), F32).at[:B].set(c).at[B].set(c_ctx)
    bn = _pick(D6, 1024)
    out = pl.pallas_call(
        _adaln_kernel,
        grid=(L, D6 // bn),
        in_specs=[pl.BlockSpec((MOD_ROWS, D), lambda l, j: (0, 0)),
                  pl.BlockSpec((1, D, bn), lambda l, j: (l, 0, j)),
                  pl.BlockSpec((1, 1, bn), lambda l, j: (l, 0, j))],
        out_specs=pl.BlockSpec((1, MOD_ROWS, bn), lambda l, j: (l, 0, j)),
        out_shape=jax.ShapeDtypeStruct((L, MOD_ROWS, D6), F32),
        compiler_params=_cparams("parallel", "parallel"),
        name="adaln",
    )(cond, mod_w, mod_b.reshape(L, 1, D6))
    return out.reshape(L * MOD_ROWS * 6, 1, D)


class _Rows:
    def __init__(self, B, S, Tc):
        self.B, self.S, self.Tc = B, S, Tc
        self.ML, self.MC = B * S, B * Tc
        self.M = self.ML + self.MC

    def tile(self, pref):
        return _pick(math.gcd(self.S, self.MC), pref)

    def mod_index(self, layer, which, row):
        mi = jnp.where(row >= self.ML, self.B, row // self.S)
        return (layer * MOD_ROWS + mi) * 6 + which


def _norm_mod(h_ref, g_ref, sc_ref, sh_ref):
    x = h_ref[...]
    y = x * lax.rsqrt(jnp.mean(x * x, axis=-1, keepdims=True) + EPS)
    return (y * g_ref[...]) * (1.0 + sc_ref[0]) + sh_ref[0]


class _PreNorm:
    def __init__(self, h, g, mod, rows, layer, sh_idx, sc_idx):
        self.h, self.g, self.mod, self.rows = h, g, mod, rows
        self.layer, self.sh_idx, self.sc_idx = layer, sh_idx, sc_idx

    def specs(self, bm):
        D = self.h.shape[1]
        rows, layer = self.rows, self.layer
        specs = [pl.BlockSpec((bm, D), lambda i, j: (i, 0)),
                 pl.BlockSpec((1, D), lambda i, j: (0, 0)),
                 pl.BlockSpec((1, 1, D), lambda i, j: (rows.mod_index(layer, self.sc_idx, i * bm), 0, 0)),
                 pl.BlockSpec((1, 1, D), lambda i, j: (rows.mod_index(layer, self.sh_idx, i * bm), 0, 0))]
        return specs, [self.h, self.g.reshape(1, D), self.mod, self.mod]


def _norm_router_kernel(h_ref, g_ref, sc_ref, sh_ref, rhi_ref, rlo_ref, o_ref, route_ref, *, n_experts):
    n = _norm_mod(h_ref, g_ref, sc_ref, sh_ref)
    o_ref[...] = n
    nh = n.astype(BF16)
    nl = (n - nh.astype(F32)).astype(BF16)
    lg = _dot(nh, rhi_ref[...]) + (_dot(nl, rhi_ref[...]) + _dot(nh, rlo_ref[...]))
    lane = lax.broadcasted_iota(jnp.int32, lg.shape, 1).astype(F32)
    neg = jnp.float32(-jnp.inf)
    lg = jnp.where(lane < n_experts, lg, neg)
    m1 = jnp.max(lg, axis=-1, keepdims=True)
    i1 = jnp.min(jnp.where(lg == m1, lane, float(LANES)), axis=-1, keepdims=True)
    sel1 = lane == i1
    lg2 = jnp.where(sel1, neg, lg)
    m2 = jnp.max(lg2, axis=-1, keepdims=True)
    i2 = jnp.min(jnp.where(lg2 == m2, lane, float(LANES)), axis=-1, keepdims=True)
    e2 = jnp.exp(m2 - m1)
    w1 = 1.0 / (1.0 + e2)
    w2 = e2 / (1.0 + e2)
    route_ref[...] = jnp.where(lane == 0.0, i1, jnp.where(lane == 1.0, i2,
                               jnp.where(lane == 2.0, w1, jnp.where(lane == 3.0, w2, 0.0))))


def _norm_router(h, g, mod, rows, layer, sh_idx, sc_idx, m_act, router):
    D = h.shape[1]
    bm = rows.tile(512)
    row_spec = pl.BlockSpec((bm, D), lambda i: (i, 0))
    in_specs = [row_spec,
                pl.BlockSpec((1, D), lambda i: (0, 0)),
                pl.BlockSpec((1, 1, D), lambda i: (rows.mod_index(layer, sc_idx, i * bm), 0, 0)),
                pl.BlockSpec((1, 1, D), lambda i: (rows.mod_index(layer, sh_idx, i * bm), 0, 0))]
    args = [h, g.reshape(1, D), mod, mod]
    n_experts = router.shape[1]
    rpad = jnp.zeros((D, LANES), F32).at[:, :n_experts].set(router)
    rhi = rpad.astype(BF16)
    rlo = (rpad - rhi.astype(F32)).astype(BF16)
    full = pl.BlockSpec((D, LANES), lambda i: (0, 0))
    return pl.pallas_call(
        functools.partial(_norm_router_kernel, n_experts=n_experts),
        grid=(m_act // bm,), in_specs=in_specs + [full, full],
        out_specs=[row_spec, pl.BlockSpec((bm, LANES), lambda i: (i, 0))],
        out_shape=[jax.ShapeDtypeStruct((m_act, D), F32), jax.ShapeDtypeStruct((m_act, LANES), F32)],
        compiler_params=_cparams("parallel"), name="norm_mod_router",
    )(*args, rhi, rlo)


def _final_norm_kernel(h_ref, g_ref, o_ref):
    x = h_ref[...]
    y = x * lax.rsqrt(jnp.mean(x * x, axis=-1, keepdims=True) + EPS)
    o_ref[...] = y * g_ref[...]


def _final_norm(h, g, m_act):
    D = h.shape[1]
    bm = _pick(m_act, 512)
    return pl.pallas_call(
        _final_norm_kernel, grid=(m_act // bm,),
        in_specs=[pl.BlockSpec((bm, D), lambda i: (i, 0)), pl.BlockSpec((1, D), lambda i: (0, 0))],
        out_specs=pl.BlockSpec((bm, D), lambda i: (i, 0)),
        out_shape=jax.ShapeDtypeStruct((m_act, D), F32),
        compiler_params=_cparams("parallel"), name="final_norm",
    )(h, g.reshape(1, D))


def _mm_kernel(a_ref, w_ref, o_ref, *acc, nk):
    if nk == 1:
        o_ref[...] = _dot(a_ref[...], w_ref[...]).astype(o_ref.dtype)
        return
    acc_ref, = acc
    k = pl.program_id(2)

    @pl.when(k == 0)
    def _():
        acc_ref[...] = jnp.zeros_like(acc_ref)

    acc_ref[...] += _dot(a_ref[...], w_ref[...])

    @pl.when(k == nk - 1)
    def _():
        o_ref[...] = acc_ref[...].astype(o_ref.dtype)


def _stacked(w):
    return w if isinstance(w, tuple) else (w[None], 0)


def _prenorm_mm_kernel(h_ref, g_ref, sc_ref, sh_ref, w_ref, o_ref, n_ref):
    @pl.when(pl.program_id(1) == 0)
    def _():
        n_ref[...] = _norm_mod(h_ref, g_ref, sc_ref, sh_ref).astype(BF16)

    o_ref[...] = _dot(n_ref[...], w_ref[...]).astype(o_ref.dtype)


def _prenorm_mm(a, w, out_dtype, m_act, bn):
    w, wl = _stacked(w)
    _, D, N = w.shape
    bm, bn = a.rows.tile(1024), _pick(N, bn)
    specs, args = a.specs(bm)
    return pl.pallas_call(
        _prenorm_mm_kernel,
        grid=(m_act // bm, N // bn),
        in_specs=specs + [pl.BlockSpec((None, D, bn), lambda i, j: (wl, 0, j))],
        out_specs=pl.BlockSpec((bm, bn), lambda i, j: (i, j)),
        out_shape=jax.ShapeDtypeStruct((m_act, N), out_dtype),
        scratch_shapes=[pltpu.VMEM((bm, D), BF16)],
        compiler_params=_cparams("parallel", "arbitrary"), name="norm_mm",
    )(*args, w)


def _mm(a, w, out_dtype, m_act, bm=1024, bn=1024, bk=None):
    if isinstance(a, _PreNorm):
        return _prenorm_mm(a, w, out_dtype, m_act, bn)
    w, wl = _stacked(w)
    _, K, N = w.shape
    bm, bn = _pick(m_act, bm), _pick(N, bn)
    bk = K if bk is None else _pick(K, bk)
    nk = K // bk
    return pl.pallas_call(
        functools.partial(_mm_kernel, nk=nk),
        grid=(N // bn, m_act // bm, nk),
        in_specs=[pl.BlockSpec((bm, bk), lambda j, i, k: (i, k)),
                  pl.BlockSpec((None, bk, bn), lambda j, i, k: (wl, k, j))],
        out_specs=pl.BlockSpec((bm, bn), lambda j, i, k: (i, j)),
        out_shape=jax.ShapeDtypeStruct((m_act, N), out_dtype),
        scratch_shapes=[pltpu.VMEM((bm, bn), F32)] if nk > 1 else [],
        compiler_params=_cparams("parallel", "parallel", "arbitrary"), name="mm",
    )(a, w)


def _mm_res_kernel(a_ref, w_ref, h_ref, gate_ref, o_ref, *acc, nk):
    if nk == 1:
        o_ref[...] = h_ref[...] + gate_ref[0] * _dot(a_ref[...], w_ref[...])
        return
    acc_ref, = acc
    k = pl.program_id(2)

    @pl.when(k == 0)
    def _():
        acc_ref[...] = jnp.zeros_like(acc_ref)

    acc_ref[...] += _dot(a_ref[...], w_ref[...])

    @pl.when(k == nk - 1)
    def _():
        o_ref[...] = h_ref[...] + gate_ref[0] * acc_ref[...]


def _mm_res(a, w, h, mod, rows, layer, gate_idx, m_rows, h_row_off=0, bm=1024, bn=1024, bk=None):
    w, wl = _stacked(w)
    _, K, N = w.shape
    bm, bn = _pick(math.gcd(rows.S, rows.MC, m_rows, h_row_off or m_rows), bm), _pick(N, bn)
    bk = K if bk is None else _pick(K, bk)
    nk = K // bk
    off = h_row_off // bm
    return pl.pallas_call(
        functools.partial(_mm_res_kernel, nk=nk),
        grid=(N // bn, m_rows // bm, nk),
        in_specs=[pl.BlockSpec((bm, bk), lambda j, i, k: (i, k)),
                  pl.BlockSpec((None, bk, bn), lambda j, i, k: (wl, k, j)),
                  pl.BlockSpec((bm, bn), lambda j, i, k: (i + off, j)),
                  pl.BlockSpec((1, 1, bn),
                               lambda j, i, k: (rows.mod_index(layer, gate_idx, (i + off) * bm), 0, j))],
        out_specs=pl.BlockSpec((bm, bn), lambda j, i, k: (i + off, j)),
        out_shape=jax.ShapeDtypeStruct(h.shape, F32),
        scratch_shapes=[pltpu.VMEM((bm, bn), F32)] if nk > 1 else [],
        input_output_aliases={2: 0},
        compiler_params=_cparams("parallel", "parallel", "arbitrary"), name="mm_residual",
    )(a, w, h, mod)


def _prenorm_swiglu_kernel(h_ref, g_ref, sc_ref, sh_ref, wg_ref, wu_ref, o_ref, n_ref):
    @pl.when(pl.program_id(1) == 0)
    def _():
        n_ref[...] = _norm_mod(h_ref, g_ref, sc_ref, sh_ref).astype(BF16)

    a = n_ref[...]
    o_ref[...] = (_silu(_dot(a, wg_ref[...])) * _dot(a, wu_ref[...])).astype(o_ref.dtype)


def _swiglu(n, w_gu, m_act, bm=1024, bn=512):
    w_gu, wl = _stacked(w_gu)
    _, D, F2 = w_gu.shape
    F = F2 // 2
    bn = _pick(F, bn)
    nf = F // bn
    bm = n.rows.tile(bm)
    specs, args = n.specs(bm)
    return pl.pallas_call(
        _prenorm_swiglu_kernel,
        grid=(m_act // bm, nf),
        in_specs=specs + [pl.BlockSpec((None, D, bn), lambda i, j: (wl, 0, j)),
                          pl.BlockSpec((None, D, bn), lambda i, j: (wl, 0, nf + j))],
        out_specs=pl.BlockSpec((bm, bn), lambda i, j: (i, j)),
        out_shape=jax.ShapeDtypeStruct((m_act, F), BF16),
        scratch_shapes=[pltpu.VMEM((bm, D), BF16)],
        compiler_params=_cparams("parallel", "arbitrary"), name="norm_swiglu",
    )(*args, w_gu, w_gu)


MOE_TILE = 1024
GATHER_ROWS = 256


def _moe_plan(route, n_experts):
    m = route.shape[0]
    P, TM, E = TOP_K * m, MOE_TILE, n_experts
    e_flat = route[:, :TOP_K].astype(jnp.int32).reshape(P)
    onehot = (e_flat[:, None] == jnp.arange(E, dtype=jnp.int32)[None, :]).astype(jnp.int32)
    incl = jnp.cumsum(onehot, axis=0)
    rank = jnp.sum((incl - onehot) * onehot, axis=1)
    ptiles = (incl[-1] + TM - 1) // TM
    tile_end = jnp.cumsum(ptiles)
    dest = (tile_end - ptiles)[e_flat] * TM + rank
    n_tiles = P // TM + E
    tile_expert = jnp.minimum(jnp.searchsorted(tile_end, jnp.arange(n_tiles), side="right"), E - 1).astype(jnp.int32)
    tok = jnp.arange(P, dtype=jnp.int32) // TOP_K
    src = (jnp.arange(n_tiles * TM, dtype=jnp.int32) % m).at[dest].set(tok)
    return src, dest.astype(jnp.int32), tile_expert, tile_end[-1:].astype(jnp.int32)


def _row_gather_pipeline(issue, wait, n_steps):
    i = pl.program_id(0)
    slot = i % 2

    @pl.when(i == 0)
    def _():
        issue(False, 0)

    @pl.when(i + 1 < n_steps)
    def _():
        issue(True, 1 - slot)

    wait(slot)
    return slot


def _gather_kernel(idx_ref, nxt_ref, src_ref, o_ref, buf_ref, sem, *, G, n_steps):
    def copy(ref, r, s):
        return pltpu.make_async_copy(src_ref.at[pl.ds(ref[0, 0, r], 1)], buf_ref.at[s, pl.ds(r, 1)], sem.at[s])

    def issue(ahead, s):
        ref = nxt_ref if ahead else idx_ref

        def body(r, carry):
            copy(ref, r, s).start()
            return carry

        lax.fori_loop(0, G, body, 0, unroll=8)

    def wait(s):
        def body(r, carry):
            copy(idx_ref, r, s).wait()
            return carry

        lax.fori_loop(0, G, body, 0, unroll=8)

    slot = _row_gather_pipeline(issue, wait, n_steps)
    o_ref[...] = buf_ref[slot].astype(o_ref.dtype)


def _gather_rows(x, idx):
    D = x.shape[1]
    G = GATHER_ROWS
    R = idx.shape[0]
    n = R // G
    smem = lambda f: pl.BlockSpec((1, 1, G), f, memory_space=pltpu.SMEM)
    return pl.pallas_call(
        functools.partial(_gather_kernel, G=G, n_steps=n),
        grid=(n,),
        in_specs=[smem(lambda i: (i, 0, 0)), smem(lambda i: (jnp.minimum(i + 1, n - 1), 0, 0)),
                  pl.BlockSpec(memory_space=pl.ANY)],
        out_specs=pl.BlockSpec((G, D), lambda i: (i, 0)),
        out_shape=jax.ShapeDtypeStruct((R, D), BF16),
        scratch_shapes=[pltpu.VMEM((2, G, D), F32), pltpu.SemaphoreType.DMA((2,))],
        compiler_params=_cparams("arbitrary"), name="moe_dispatch_gather",
    )(idx.reshape(n, 1, G), idx.reshape(n, 1, G), x)


def _moe_swiglu_kernel(te_ref, nu_ref, x_ref, wg_ref, wu_ref, o_ref, wgb_ref, wub_ref):
    i = pl.program_id(1)
    active = i < nu_ref[0]

    @pl.when(jnp.logical_or(i == 0, te_ref[i] != te_ref[jnp.maximum(i - 1, 0)]))
    def _():
        wgb_ref[...] = wg_ref[...].astype(BF16)
        wub_ref[...] = wu_ref[...].astype(BF16)

    @pl.when(active)
    def _():
        a = x_ref[...]
        o_ref[...] = (_silu(_dot(a, wgb_ref[...])) * _dot(a, wub_ref[...])).astype(o_ref.dtype)

    @pl.when(jnp.logical_not(active))
    def _():
        o_ref[...] = jnp.zeros_like(o_ref)


def _moe_down_kernel(te_ref, nu_ref, a_ref, w_ref, o_ref):
    active = pl.program_id(1) < nu_ref[0]

    @pl.when(active)
    def _():
        o_ref[...] = _dot(a_ref[...], w_ref[...])

    @pl.when(jnp.logical_not(active))
    def _():
        o_ref[...] = jnp.zeros_like(o_ref)


def _moe_experts(xs, tile_expert, n_used, w_gu, w_down, m, bn=512, bn_down=1024):
    _, E, D, F2 = w_gu.shape
    F = F2 // 2
    R = xs.shape[0]
    TM = MOE_TILE
    bn = _pick(F, bn)
    nf = F // bn
    h1 = pl.pallas_call(
        _moe_swiglu_kernel,
        grid_spec=pltpu.PrefetchScalarGridSpec(
            num_scalar_prefetch=2, grid=(nf, R // TM),
            in_specs=[pl.BlockSpec((TM, D), lambda j, i, te, nu: (i, 0)),
                      pl.BlockSpec((None, None, D, bn), lambda j, i, te, nu: (m, te[i], 0, j)),
                      pl.BlockSpec((None, None, D, bn), lambda j, i, te, nu: (m, te[i], 0, nf + j))],
            out_specs=pl.BlockSpec((TM, bn), lambda j, i, te, nu: (i, j)),
            scratch_shapes=[pltpu.VMEM((D, bn), BF16), pltpu.VMEM((D, bn), BF16)]),
        out_shape=jax.ShapeDtypeStruct((R, F), BF16),
        compiler_params=_cparams("arbitrary", "arbitrary"), name="moe_swiglu",
    )(tile_expert, n_used, xs, w_gu, w_gu)
    bnd = _pick(D, bn_down)
    return pl.pallas_call(
        _moe_down_kernel,
        grid_spec=pltpu.PrefetchScalarGridSpec(
            num_scalar_prefetch=2, grid=(D // bnd, R // TM),
            in_specs=[pl.BlockSpec((TM, F), lambda j, i, te, nu: (i, 0)),
                      pl.BlockSpec((None, None, F, bnd), lambda j, i, te, nu: (m, te[i], 0, j),
                                   pipeline_mode=pl.Buffered(1))],
            out_specs=pl.BlockSpec((TM, bnd), lambda j, i, te, nu: (i, j))),
        out_shape=jax.ShapeDtypeStruct((R, D), F32),
        compiler_params=_cparams("parallel", "arbitrary"), name="moe_down",
    )(tile_expert, n_used, h1, w_down)


def _combine_kernel(idx_ref, nxt_ref, y_ref, route_ref, h_ref, gate_ref, *rest, G, n_steps, norm):
    if norm == "mod":
        g_ref, sc_ref, sh_ref, o_ref, n_ref, buf_ref, sem = rest
    else:
        g_ref, n_ref, buf_ref, sem = rest

    def copy(ref, r, k, s):
        return pltpu.make_async_copy(y_ref.at[pl.ds(ref[0, 0, TOP_K * r + k], 1)],
                                     buf_ref.at[s, k, pl.ds(r, 1)], sem.at[s])

    def issue(ahead, s):
        ref = nxt_ref if ahead else idx_ref

        def body(r, carry):
            for k in range(TOP_K):
                copy(ref, r, k, s).start()
            return carry

        lax.fori_loop(0, G, body, 0, unroll=4)

    def wait(s):
        def body(r, carry):
            for k in range(TOP_K):
                copy(idx_ref, r, k, s).wait()
            return carry

        lax.fori_loop(0, G, body, 0, unroll=4)

    slot = _row_gather_pipeline(issue, wait, n_steps)
    y = route_ref[:, TOP_K:TOP_K + 1] * buf_ref[slot, 0]
    for k in range(1, TOP_K):
        y = y + route_ref[:, TOP_K + k:TOP_K + k + 1] * buf_ref[slot, k]
    hn = h_ref[...] + gate_ref[0] * y
    z = (hn * lax.rsqrt(jnp.mean(hn * hn, axis=-1, keepdims=True) + EPS)) * g_ref[...]
    if norm == "mod":
        o_ref[...] = hn
        z = z * (1.0 + sc_ref[0]) + sh_ref[0]
    n_ref[...] = z.astype(n_ref.dtype)


def _moe_combine(y, dest, route, h, mod, rows, layer, gate_idx, m_act, next_norm):
    D = h.shape[1]
    G = rows.tile(GATHER_ROWS)
    n = m_act // G
    smem = lambda f: pl.BlockSpec((1, 1, TOP_K * G), f, memory_space=pltpu.SMEM)
    idx = dest.reshape(n, 1, TOP_K * G)
    row_spec = pl.BlockSpec((G, D), lambda i: (i, 0))
    vec_spec = pl.BlockSpec((1, D), lambda i: (0, 0))
    in_specs = [smem(lambda i: (i, 0, 0)), smem(lambda i: (jnp.minimum(i + 1, n - 1), 0, 0)),
                pl.BlockSpec(memory_space=pl.ANY),
                pl.BlockSpec((G, LANES), lambda i: (i, 0)),
                row_spec,
                pl.BlockSpec((1, 1, D), lambda i: (rows.mod_index(layer, gate_idx, i * G), 0, 0))]
    args = [idx, idx, y, route, h, mod]
    h_shape = jax.ShapeDtypeStruct(h.shape, F32)
    kind, g = next_norm[0], next_norm[1].reshape(1, D)
    if kind == "mod":
        nl = next_norm[2]
        in_specs += [vec_spec,
                     pl.BlockSpec((1, 1, D), lambda i: (rows.mod_index(nl, 1, i * G), 0, 0)),
                     pl.BlockSpec((1, 1, D), lambda i: (rows.mod_index(nl, 0, i * G), 0, 0))]
        args += [g, mod, mod]
        out_specs, out_shape, aliases = [row_spec, row_spec], [h_shape, jax.ShapeDtypeStruct((m_act, D), BF16)], {4: 0}
    else:
        in_specs += [vec_spec]
        args += [g]
        out_specs, out_shape, aliases = row_spec, jax.ShapeDtypeStruct((m_act, D), F32), {}
    return pl.pallas_call(
        functools.partial(_combine_kernel, G=G, n_steps=n, norm=kind),
        grid=(n,),
        in_specs=in_specs, out_specs=out_specs, out_shape=out_shape,
        scratch_shapes=[pltpu.VMEM((2, TOP_K, G, D), F32), pltpu.SemaphoreType.DMA((2,))],
        input_output_aliases=aliases,
        compiler_params=_cparams("arbitrary"), name="moe_combine",
    )(*args)


def _dft_cos_sin(n):
    k = np.arange(n, dtype=np.int64)
    ang = 2.0 * np.pi * ((k[:, None] * k[None, :]) % n).astype(np.float64) / n
    return np.cos(ang), np.sin(ang)


def _fnet_chan_kernel(u_ref, w_ref, o_ref, *, groups, gd):
    for g in range(groups):
        r = _dot(u_ref[:, g * gd:(g + 1) * gd], w_ref[...])
        o_ref[0, :, g * gd:(g + 1) * gd] = r[:, :gd].astype(o_ref.dtype)
        o_ref[1, :, g * gd:(g + 1) * gd] = r[:, gd:].astype(o_ref.dtype)


def _fnet_chan(u, m_act):
    D = u.shape[1]
    gd = FNET_GROUP_DIM
    c, s = _dft_cos_sin(gd)
    w = jnp.asarray(np.concatenate([c, -s], axis=1) / math.sqrt(gd), BF16)
    bm = _pick(m_act, 512)
    return pl.pallas_call(
        functools.partial(_fnet_chan_kernel, groups=D // gd, gd=gd),
        grid=(m_act // bm,),
        in_specs=[pl.BlockSpec((bm, D), lambda i: (i, 0)), pl.BlockSpec((gd, 2 * gd), lambda i: (0, 0))],
        out_specs=pl.BlockSpec((2, bm, D), lambda i: (0, i, 0)),
        out_shape=jax.ShapeDtypeStruct((2, m_act, D), BF16),
        compiler_params=_cparams("parallel"), name="fnet_chan_dft",
    )(u, w)


def _fft1_kernel(x_ref, w_ref, tw_ref, o_ref, *, n1, nb, D):
    x2 = x_ref[...].reshape(2 * n1, nb * D)
    y = _dot(w_ref[...], x2)
    yr, yi = y[:n1], y[n1:]
    for t in range(nb):
        twr = tw_ref[0, 0, :, t:t + 1]
        twi = tw_ref[1, 0, :, t:t + 1]
        a, b = yr[:, t * D:(t + 1) * D], yi[:, t * D:(t + 1) * D]
        o_ref[0, 0, t] = (a * twr - b * twi).astype(o_ref.dtype)
        o_ref[0, 1, t] = (a * twi + b * twr).astype(o_ref.dtype)


def _fft3_kernel(z_ref, w_ref, o_ref, *, n2, nb, D):
    for t in range(nb):
        z = z_ref[0, :, :, t * D:(t + 1) * D].reshape(2 * n2, D)
        o_ref[0, :, t * D:(t + 1) * D] = _dot(w_ref[...], z).astype(o_ref.dtype)


def _fnet_seq_latent(p, rows):
    D = p.shape[2]
    B, S = rows.B, rows.S
    n1 = 64
    n2 = S // n1
    assert n1 * n2 == S and n2 % 8 == 0 and rows.MC % n2 == 0
    nb = 8
    c1, s1 = _dft_cos_sin(n1)
    w1 = jnp.asarray(np.block([[c1, s1], [-s1, c1]]) / math.sqrt(n1), BF16)
    f = np.arange(n1)[:, None] * np.arange(n2)[None, :]
    ang = 2.0 * np.pi * f / S
    tw = np.stack([np.cos(ang), -np.sin(ang)]).reshape(2, n1, n2 // nb, nb).transpose(0, 2, 1, 3)
    tw = jnp.asarray(tw, F32)
    pv = p.reshape(2, p.shape[1] // n2, n2 * D)
    y = pl.pallas_call(
        functools.partial(_fft1_kernel, n1=n1, nb=nb, D=D),
        grid=(B, n2 // nb),
        in_specs=[pl.BlockSpec((2, n1, nb * D), lambda b, j: (0, b, j)),
                  pl.BlockSpec((2 * n1, 2 * n1), lambda b, j: (0, 0)),
                  pl.BlockSpec((2, 1, n1, nb), lambda b, j: (0, j, 0, 0))],
        out_specs=pl.BlockSpec((1, 2, nb, n1, D), lambda b, j: (b, 0, j, 0, 0)),
        out_shape=jax.ShapeDtypeStruct((B, 2, n2, n1, D), BF16),
        compiler_params=_cparams("parallel", "parallel"), name="fnet_fft_stage1",
    )(pv, w1, tw)
    c2, s2 = _dft_cos_sin(n2)
    w3 = jnp.asarray(np.concatenate([c2, s2], axis=1) / math.sqrt(n2), BF16)
    yv = y.reshape(B, 2, n2, n1 * D)
    out = pl.pallas_call(
        functools.partial(_fft3_kernel, n2=n2, nb=nb, D=D),
        grid=(B, n1 // nb),
        in_specs=[pl.BlockSpec((1, 2, n2, nb * D), lambda b, j: (b, 0, 0, j)),
                  pl.BlockSpec((n2, 2 * n2), lambda b, j: (0, 0))],
        out_specs=pl.BlockSpec((1, n2, nb * D), lambda b, j: (b, 0, j)),
        out_shape=jax.ShapeDtypeStruct((B, n2, n1 * D), BF16),
        compiler_params=_cparams("parallel", "parallel"), name="fnet_fft_stage2",
    )(yv, w3)
    return out.reshape(B * S, D)


def _dft_ctx_kernel(z_ref, w_ref, o_ref, *, T, D):
    z = z_ref[...].reshape(2 * T, D)
    o_ref[...] = _dot(w_ref[...], z).astype(o_ref.dtype)


def _fnet_seq_ctx(p, rows):
    D = p.shape[2]
    T = rows.Tc
    c, s = _dft_cos_sin(T)
    w = jnp.asarray(np.concatenate([c, s], axis=1) / math.sqrt(T), BF16)
    off = rows.ML // T
    return pl.pallas_call(
        functools.partial(_dft_ctx_kernel, T=T, D=D),
        grid=(rows.B,),
        in_specs=[pl.BlockSpec((2, T, D), lambda b: (0, off + b, 0)),
                  pl.BlockSpec((T, 2 * T), lambda b: (0, 0))],
        out_specs=pl.BlockSpec((T, D), lambda b: (b, 0)),
        out_shape=jax.ShapeDtypeStruct((rows.MC, D), BF16),
        compiler_params=_cparams("parallel"), name="fnet_dft_ctx",
    )(p, w)


def _fourier_mixer(n, h, w_in, w_out, mod, rows, layer, need_ctx):
    m_act = rows.M if need_ctx else rows.ML
    u = _mm(n, w_in, BF16, m_act)
    p = _fnet_chan(u, m_act)
    f_lat = _fnet_seq_latent(p, rows)
    h = _mm_res(f_lat, w_out, h, mod, rows, layer, 2, rows.ML)
    if need_ctx:
        f_ctx = _fnet_seq_ctx(p, rows)
        h = _mm_res(f_ctx, w_out, h, mod, rows, layer, 2, rows.MC, h_row_off=rows.ML)
    return h


def _mla_proj_kernel(z_ref, qn_ref, kvn_ref, wq_ref, wqs_ref, wkv_ref, cos_ref, sin_ref,
                     q_ref, k_ref, v_ref, *, H, qr, kvr, scale):
    def rms(x, g):
        return ((x * lax.rsqrt(jnp.mean(x * x, axis=-1, keepdims=True) + EPS)) * g).astype(BF16)

    qn = rms(z_ref[:, :qr], qn_ref[...])
    kvn = rms(z_ref[:, qr:qr + kvr], kvn_ref[...])
    kr = z_ref[:, qr + kvr:qr + kvr + LANES]
    kr_sw = z_ref[:, qr + kvr + LANES:qr + kvr + 2 * LANES]
    cos, sin = cos_ref[...], sin_ref[...]
    k_rope = (kr * cos + kr_sw * sin).astype(BF16)
    q = _dot(qn, wq_ref[...]) * scale
    q_sw = _dot(qn, wqs_ref[...]) * scale
    kv = _dot(kvn, wkv_ref[...])
    P = MLA_HEAD_PAD
    for hh in range(H):
        q_ref[0, hh, :, :LANES] = q[:, hh * P:hh * P + LANES].astype(BF16)
        q_ref[0, hh, :, LANES:] = (q[:, hh * P + LANES:(hh + 1) * P] * cos
                                   + q_sw[:, hh * LANES:(hh + 1) * LANES] * sin).astype(BF16)
        k_ref[0, hh, :, :LANES] = kv[:, hh * LANES:(hh + 1) * LANES].astype(BF16)
        k_ref[0, hh, :, LANES:] = k_rope
        v_ref[0, hh] = kv[:, (H + hh) * LANES:(H + hh + 1) * LANES].T.astype(BF16)


def _mla_attn_kernel(q_ref, k_ref, vt_ref, o_ref):
    for hh in range(q_ref.shape[1]):
        s_t = _dot_nt(k_ref[0, hh], q_ref[0, hh])
        m = jnp.max(s_t, axis=0, keepdims=True)
        p = jnp.exp2(s_t - m)
        l = jnp.sum(p, axis=0, keepdims=True)
        o_t = _dot(vt_ref[0, hh], p.astype(BF16)) / l
        o_ref[:, hh * MLA_V:(hh + 1) * MLA_V] = o_t.T.astype(o_ref.dtype)


def _axial_rope_table(S, Tc):
    half = MLA_ROPE // 2
    rows_ = S // GRID_W
    row = jnp.repeat(jnp.arange(rows_, dtype=F32), GRID_W)
    col = jnp.tile(jnp.arange(GRID_W, dtype=F32), rows_)
    n_freq = MLA_ROPE // 4
    inv = ROPE_BASE ** (-jnp.arange(n_freq, dtype=F32) / n_freq)
    ang = jnp.concatenate([row[:, None] * inv, col[:, None] * inv], axis=-1)
    cos, sin = jnp.cos(ang), jnp.sin(ang)
    zpad = jnp.zeros((S, LANES - 2 * half), F32)
    cos_l = jnp.concatenate([cos, cos, zpad], axis=-1)
    sin_l = jnp.concatenate([sin, sin, zpad], axis=-1)
    cos_c = jnp.concatenate([jnp.ones((Tc, 2 * half), F32), jnp.zeros((Tc, LANES - 2 * half), F32)], axis=-1)
    return jnp.concatenate([cos_l, cos_c], axis=0), jnp.concatenate([sin_l, jnp.zeros((Tc, LANES), F32)], axis=0)


def _swap_halves_neg(w):
    half = w.shape[-1] // 2
    return jnp.concatenate([-w[..., half:], w[..., :half]], axis=-1)


def _mla_mixer(n, h, w_in, q_norm, w_uq, kv_norm, w_ukv, w_o, mod, rows, layer):
    B, S, Tc = rows.B, rows.S, rows.Tc
    D = h.shape[1]
    qr, kvr = q_norm.shape[0], kv_norm.shape[0]
    H = w_uq.shape[1] // (MLA_NOPE + MLA_ROPE)
    C = CHUNK
    assert Tc == C and S % C == 0 and MLA_NOPE == LANES and MLA_V == LANES
    pad = LANES - MLA_ROPE
    w_kr = w_in[:, qr + kvr:]
    zc = jnp.zeros((D, pad), F32)
    w_in_p = jnp.concatenate([w_in[:, :qr + kvr], w_kr, zc, _swap_halves_neg(w_kr), zc], axis=1).astype(BF16)
    wq = w_uq.reshape(qr, H, MLA_NOPE + MLA_ROPE)
    zq = jnp.zeros((qr, H, pad), F32)
    wq_p = jnp.concatenate([wq, zq], axis=-1).reshape(qr, H * MLA_HEAD_PAD).astype(BF16)
    wq_sw = jnp.concatenate([_swap_halves_neg(wq[..., MLA_NOPE:]), zq], axis=-1).reshape(qr, H * LANES).astype(BF16)
    wkv = w_ukv.reshape(kvr, H, MLA_NOPE + MLA_V)
    wkv_p = jnp.concatenate([wkv[..., :MLA_NOPE].reshape(kvr, H * LANES),
                             wkv[..., MLA_NOPE:].reshape(kvr, H * LANES)], axis=1).astype(BF16)
    cos_t, sin_t = _axial_rope_table(S, Tc)

    z = _mm(n, w_in_p, F32, rows.M)
    nS = S // C
    n_lat_tiles = rows.ML // C
    T = S + Tc

    def bt(i):
        lat = i < n_lat_tiles
        return jnp.where(lat, i // nS, i - n_lat_tiles), jnp.where(lat, i % nS, nS)

    def kv_map(i):
        b, t = bt(i)
        return (b, 0, t, 0)

    ZW = w_in_p.shape[1]
    const = lambda shape: pl.BlockSpec(shape, lambda i: (0,) * len(shape))
    q, k, v_t = pl.pallas_call(
        functools.partial(_mla_proj_kernel, H=H, qr=qr, kvr=kvr,
                          scale=float((MLA_NOPE + MLA_ROPE) ** -0.5) * math.log2(math.e)),
        grid=(rows.M // C,),
        in_specs=[pl.BlockSpec((C, ZW), lambda i: (i, 0)),
                  const((1, qr)), const((1, kvr)),
                  const(wq_p.shape), const(wq_sw.shape), const(wkv_p.shape),
                  pl.BlockSpec((C, LANES), lambda i: (bt(i)[1], 0)),
                  pl.BlockSpec((C, LANES), lambda i: (bt(i)[1], 0))],
        out_specs=[pl.BlockSpec((1, H, C, MLA_HEAD_PAD), kv_map),
                   pl.BlockSpec((1, H, C, MLA_HEAD_PAD), kv_map),
                   pl.BlockSpec((1, H, MLA_V, C), lambda i: (bt(i)[0], 0, 0, bt(i)[1]))],
        out_shape=[jax.ShapeDtypeStruct((B, H, T, MLA_HEAD_PAD), BF16),
                   jax.ShapeDtypeStruct((B, H, T, MLA_HEAD_PAD), BF16),
                   jax.ShapeDtypeStruct((B, H, MLA_V, T), BF16)],
        compiler_params=_cparams("parallel"), name="mla_project",
    )(z, q_norm.reshape(1, qr), kv_norm.reshape(1, kvr), wq_p, wq_sw, wkv_p, cos_t, sin_t)

    bq = _pick(S, 512)
    nq = S // bq
    HB = 2 if H % 2 == 0 else 1
    o_lat = pl.pallas_call(
        _mla_attn_kernel,
        grid=(B, H // HB, nq),
        in_specs=[pl.BlockSpec((1, HB, bq, MLA_HEAD_PAD), lambda b, hh, i: (b, hh, i, 0)),
                  pl.BlockSpec((1, HB, T, MLA_HEAD_PAD), lambda b, hh, i: (b, hh, 0, 0)),
                  pl.BlockSpec((1, HB, MLA_V, T), lambda b, hh, i: (b, hh, 0, 0))],
        out_specs=pl.BlockSpec((bq, HB * MLA_V), lambda b, hh, i: (b * nq + i, hh)),
        out_shape=jax.ShapeDtypeStruct((rows.ML, H * MLA_V), BF16),
        compiler_params=_cparams("parallel", "parallel", "arbitrary"), name="mla_attention",
    )(q, k, v_t)
    o_ctx = pl.pallas_call(
        _mla_attn_kernel,
        grid=(B, H),
        in_specs=[pl.BlockSpec((1, 1, Tc, MLA_HEAD_PAD), lambda b, hh: (b, hh, nS, 0)),
                  pl.BlockSpec((1, 1, Tc, MLA_HEAD_PAD), lambda b, hh: (b, hh, nS, 0)),
                  pl.BlockSpec((1, 1, MLA_V, Tc), lambda b, hh: (b, hh, 0, nS))],
        out_specs=pl.BlockSpec((Tc, MLA_V), lambda b, hh: (b, hh)),
        out_shape=jax.ShapeDtypeStruct((rows.MC, H * MLA_V), BF16),
        compiler_params=_cparams("parallel", "parallel"), name="mla_attention_ctx",
    )(q, k, v_t)
    h = _mm_res(o_lat, w_o, h, mod, rows, layer, 2, rows.ML)
    return _mm_res(o_ctx, w_o, h, mod, rows, layer, 2, rows.MC, h_row_off=rows.ML)


def _ret_kernel(dr_ref, q_ref, k_ref, v_ref, cos_ref, sin_ref, *rest, H, C, dk, dv, backward, combine):
    if combine:
        yf_ref, gf_ref, gb_ref, o_ref, R_ref, D_ref, qd_ref, kd_ref, cd_ref = rest
    else:
        o_ref, R_ref, D_ref, qd_ref, kd_ref, cd_ref = rest
    b, s = pl.program_id(0), pl.program_id(1)
    half = dk // 2

    @pl.when((b == 0) & (s == 0))
    def _():
        ri = lax.broadcasted_iota(jnp.int32, (C, C), 0).astype(F32)
        ci = lax.broadcasted_iota(jnp.int32, (C, C), 1).astype(F32)
        diff = (ci - ri) if backward else (ri - ci)
        r = lax.broadcasted_iota(jnp.int32, (C, LANES), 0).astype(F32)
        qpow = (C - r) if backward else (r + 1.0)
        kpow = r if backward else (C - 1.0 - r)
        for hh in range(H):
            lg = jnp.log1p(-jnp.exp(dr_ref[hh]))
            D_ref[hh] = jnp.where(diff >= 0, jnp.exp(jnp.maximum(diff, 0.0) * lg[0:1, :]), 0.0)
            qd_ref[hh] = jnp.exp(qpow * lg[0:1, :LANES])
            kd_ref[hh] = jnp.exp(kpow * lg[0:1, :LANES])
            cd_ref[hh] = jnp.exp(float(C) * lg[:, :LANES])

    @pl.when(s == 0)
    def _():
        R_ref[...] = jnp.zeros_like(R_ref)

    cos, sin = cos_ref[...], sin_ref[...]

    def rope(x):
        x1, x2 = x[:, :half], x[:, half:]
        return jnp.concatenate([x1 * cos - x2 * sin, x1 * sin + x2 * cos], axis=-1)

    for hh in range(H):
        q = rope(q_ref[:, hh * dk:(hh + 1) * dk].astype(F32))
        k = rope(k_ref[:, hh * dk:(hh + 1) * dk].astype(F32)) * (dk ** -0.5)
        v = v_ref[:, hh * dv:(hh + 1) * dv]
        qd = jnp.concatenate([qd_ref[hh]] * (dk // LANES), axis=-1)
        kd = jnp.concatenate([kd_ref[hh]] * (dk // LANES), axis=-1)
        cd = jnp.concatenate([cd_ref[hh][0:1]] * (dv // LANES), axis=-1)
        a = _dot_nt(q.astype(BF16), k.astype(BF16)) * D_ref[hh]
        R = R_ref[hh]
        o = _dot(a.astype(BF16), v) + _dot((q * qd).astype(BF16), R.astype(BF16))
        R_ref[hh] = R * cd + _dot_tn((k * kd).astype(BF16), v)
        mu = jnp.mean(o, axis=-1, keepdims=True)
        d = o - mu
        y = d * lax.rsqrt(jnp.mean(d * d, axis=-1, keepdims=True) + GN_EPS)
        sl = slice(hh * dv, (hh + 1) * dv)
        if combine:
            y = (_silu(gf_ref[:, sl].astype(F32)) * yf_ref[:, sl].astype(F32)
                 + _silu(gb_ref[:, sl].astype(F32)) * y)
        o_ref[:, sl] = y.astype(o_ref.dtype)


def _retention_rope_table(S, Tc, dim):
    inv = ROPE_BASE ** (-jnp.linspace(0.0, 1.0, dim // 2, dtype=F32))
    ang = jnp.arange(S, dtype=F32)[:, None] * inv
    cos = jnp.concatenate([jnp.cos(ang), jnp.ones((Tc, dim // 2), F32)], axis=0)
    sin = jnp.concatenate([jnp.sin(ang), jnp.zeros((Tc, dim // 2), F32)], axis=0)
    return cos, sin


def _retention_mixer(n, h, w_in, decay_rate, w_o, mod, rows, layer):
    B, S, Tc = rows.B, rows.S, rows.Tc
    D = h.shape[1]
    H = RET_HEADS
    dk = D // H
    dv = 2 * dk
    QK, VT = H * dk, H * dv
    C = CHUNK
    assert Tc == C and S % C == 0 and dk // 2 == LANES and w_in[0].shape[2] == 2 * QK + 3 * VT and VT == 2 * QK
    z = _mm(n, w_in, BF16, rows.M)
    cos_t, sin_t = _retention_rope_table(S, Tc, dk)
    nS = S // C
    n_lat_tiles = rows.ML // C
    dr = jnp.broadcast_to(decay_rate.astype(F32)[:, :, None, None], (2, H, 8, C))

    def run(backward, yf):
        def blk(b, s):
            j = (nS - s) if backward else (s - 1)
            return jnp.where(s == 0, n_lat_tiles + b, b * nS + j)

        def tab(b, s):
            j = (nS - s) if backward else (s - 1)
            return (jnp.where(s == 0, nS, j), 0)

        in_specs = [pl.BlockSpec((H, 8, C), lambda b, s: (0, 0, 0)),
                    pl.BlockSpec((C, QK), lambda b, s: (blk(b, s), 0)),
                    pl.BlockSpec((C, QK), lambda b, s: (blk(b, s), 1)),
                    pl.BlockSpec((C, VT), lambda b, s: (blk(b, s), 1)),
                    pl.BlockSpec((C, LANES), tab),
                    pl.BlockSpec((C, LANES), tab)]
        args = [dr[1 if backward else 0], z, z, z, cos_t, sin_t]
        if yf is not None:
            in_specs += [pl.BlockSpec((C, VT), lambda b, s: (blk(b, s), 0)),
                         pl.BlockSpec((C, VT), lambda b, s: (blk(b, s), 2)),
                         pl.BlockSpec((C, VT), lambda b, s: (blk(b, s), 3))]
            args += [yf, z, z]
        return pl.pallas_call(
            functools.partial(_ret_kernel, H=H, C=C, dk=dk, dv=dv, backward=backward, combine=yf is not None),
            grid=(B, nS + 1),
            in_specs=in_specs,
            out_specs=pl.BlockSpec((C, VT), lambda b, s: (blk(b, s), 0)),
            out_shape=jax.ShapeDtypeStruct((rows.M, VT), BF16),
            scratch_shapes=[pltpu.VMEM((H, dk, dv), F32), pltpu.VMEM((H, C, C), F32),
                            pltpu.VMEM((H, C, LANES), F32), pltpu.VMEM((H, C, LANES), F32),
                            pltpu.VMEM((H, 8, LANES), F32)],
            compiler_params=_cparams("arbitrary", "arbitrary"),
            name="retention_bwd_combine" if backward else "retention_fwd",
        )(*args)

    y_f = run(False, None)
    y = run(True, y_f)
    return _mm_res(y, w_o, h, mod, rows, layer, 2, rows.M, bm=512)


def _ffn(n, h, w_gu, w_down, mod, rows, layer, m_act):
    h1 = _swiglu(n, w_gu, m_act)
    return _mm_res(h1, w_down, h, mod, rows, layer, 5, m_act, bk=w_down[0].shape[1] // 2)


def _moe_ffn(n, route, h, w_gu, w_down, m, mod, rows, layer, m_act, next_norm):
    src, dest, tile_expert, n_used = _moe_plan(route, w_gu.shape[1])
    xs = _gather_rows(n, src)
    y = _moe_experts(xs, tile_expert, n_used, w_gu, w_down, m)
    return _moe_combine(y, dest, route, h, mod, rows, layer, 5, m_act, next_norm)


def kernel(x, c, ctx, c_ctx, mod_w, mod_b, norm_g, final_norm_g, fnet_w_in, fnet_w_out, mla_w_in, mla_q_norm, mla_w_uq, mla_kv_norm, mla_w_ukv, mla_w_o, ret_w_in, ret_decay_rate, ret_w_o, ffn_w_gu, ffn_w_down, moe_router, moe_w_gu, moe_w_down):
    B, S, D = x.shape
    Tc = ctx.shape[1]
    depth = mod_w.shape[0]
    rows = _Rows(B, S, Tc)
    h = jnp.concatenate([x.reshape(rows.ML, D), ctx.reshape(rows.MC, D)], axis=0)
    mod = _adaln(c, c_ctx, mod_w, mod_b)
    fnet_w_in, fnet_w_out, ret_w_in, ret_w_o, mla_w_o, ffn_w_gu, ffn_w_down, moe_w_down = (
        w.astype(BF16) for w in (fnet_w_in, fnet_w_out, ret_w_in, ret_w_o, mla_w_o, ffn_w_gu, ffn_w_down,
                                 moe_w_down))
    n_pre = None
    for i in range(depth):
        need_ctx = i < depth - 1
        kind, j = i % 3, i // 3
        m_act = rows.M if (need_ctx or kind != 0) else rows.ML
        n = n_pre if n_pre is not None else _PreNorm(h, norm_g[i, 0], mod, rows, i, 0, 1)
        n_pre = None
        if kind == 0:
            h = _fourier_mixer(n, h, (fnet_w_in, j), (fnet_w_out, j), mod, rows, i, need_ctx)
        elif kind == 1:
            h = _mla_mixer(n, h, mla_w_in[j], mla_q_norm[j], mla_w_uq[j], mla_kv_norm[j], mla_w_ukv[j],
                           (mla_w_o, j), mod, rows, i)
        else:
            h = _retention_mixer(n, h, (ret_w_in, j), ret_decay_rate[j], (ret_w_o, j), mod, rows, i)
        m_act = rows.M if need_ctx else rows.ML
        m = i // 2
        if i % 2 == 0:
            n = _PreNorm(h, norm_g[i, 1], mod, rows, i, 3, 4)
            h = _ffn(n, h, (ffn_w_gu, m), (ffn_w_down, m), mod, rows, i, m_act)
        else:
            n, route = _norm_router(h, norm_g[i, 1], mod, rows, i, 3, 4, m_act, moe_router[m])
            if i == depth - 1:
                return _moe_ffn(n, route, h, moe_w_gu, moe_w_down, m, mod, rows, i, m_act,
                                ("final", final_norm_g)).reshape(B, S, D)
            h, n_pre = _moe_ffn(n, route, h, moe_w_gu, moe_w_down, m, mod, rows, i, m_act,
                                ("mod", norm_g[i + 1, 0], i + 1))
    return _final_norm(h, final_norm_g, rows.ML).reshape(B, S, D)
```
